```python
import jax
import jax.numpy as jnp
from jax import lax
import numpy as np

D_MODEL = 1024
BATCH = 16
SEQ = 2048
DEPTH = 4

GRID_W = 64
CTX_LEN = 256
N_EVEN = (DEPTH + 1) // 2
N_ODD = DEPTH // 2
NORM_EPS = 1e-6

A_HEAD_DIM = 64
A_HEADS = D_MODEL // (2 * A_HEAD_DIM)
A_WIDTH = A_HEADS * A_HEAD_DIM
A_DECAY_RANK = 64
A_ICLR_RANK = 64
A_GATE_RANK = 128
A_DECAY_SCALE = 0.606531
A_LN_EPS = 64e-5
A_COLS = 3 * A_WIDTH + 2 * A_DECAY_RANK + 2 * A_ICLR_RANK + A_GATE_RANK

B_HEADS = D_MODEL // 128
B_NOPE = 64
B_ROPE = 32
B_QK = B_NOPE + B_ROPE
B_V = 64
B_WIDTH = B_HEADS * B_V
B_Q_RANK = D_MODEL // 4
B_KV_RANK = D_MODEL // 8
B_COLS = B_Q_RANK + B_KV_RANK + B_ROPE
ATTN_SCALE = B_QK ** -0.5
ROPE_BASE = 10000.0
Q_BLOCK = 128

IN_COLS = A_COLS + B_COLS
MIX_WIDTH = A_WIDTH + B_WIDTH

C_GROUPS = 8
C_GROUP_DIM = D_MODEL // C_GROUPS

N_EXPERTS = 16
N_GROUPS = 4
EXPERTS_PER_GROUP = N_EXPERTS // N_GROUPS
TOP_K = 2
D_EXPERT = D_MODEL // 2
D_SHARED = D_MODEL // 2
MOE_BLOCK = 128

kernel_name = 'hybrid_rwkv7_mla_fnet_moe_dit'


def rms_norm(x, g, eps=NORM_EPS):
    xf = x.astype(jnp.float32)
    y = xf * lax.rsqrt(jnp.mean(xf * xf, axis=-1, keepdims=True) + eps)
    return (y * g.astype(jnp.float32)).astype(x.dtype)


def modulate(x, g, shift, scale):
    return rms_norm(x, g) * (1 + scale) + shift


def centred_shift(p, mu):
    zeros = jnp.zeros_like(p[:, :1])
    prev = jnp.concatenate([zeros, p[:, :-1]], axis=1)
    nxt = jnp.concatenate([p[:, 1:], zeros], axis=1)
    return p + mu * (0.5 * (prev + nxt) - p)


def rwkv7_scan(s0, r, w, k, v, a_vec, b_vec, reverse):
    def step(s, xs):
        r_t, w_t, k_t, v_t, a_t, b_t = xs
        sa = jnp.einsum('bhvk,bhk->bhv', s, a_t)
        s = s * w_t[:, :, None, :] + sa[..., None] * b_t[:, :, None, :] + v_t[..., None] * k_t[:, :, None, :]
        return s, jnp.einsum('bhvk,bhk->bhv', s, r_t)
    xs = (jnp.swapaxes(r, 0, 1), jnp.swapaxes(w, 0, 1), jnp.swapaxes(k, 0, 1),
          jnp.swapaxes(v, 0, 1), jnp.swapaxes(a_vec, 0, 1), jnp.swapaxes(b_vec, 0, 1))
    s, ys = lax.scan(step, s0, xs, reverse=reverse)
    return s, jnp.swapaxes(ys, 0, 1)


def heads_a(t):
    return t.reshape(t.shape[:-1] + (A_HEADS, A_HEAD_DIM))


def rwkv7_mix(p, s0, mu, w0, w2, a0, a2, g2, k_k, k_a, r_k, ln_w, ln_b):
    dt = p.dtype
    bsz, t_len, _ = p.shape
    p = centred_shift(p, mu).astype(jnp.float32)
    o1, o2, o3 = A_WIDTH, 2 * A_WIDTH, 3 * A_WIDTH
    o4 = o3 + 2 * A_DECAY_RANK
    o5 = o4 + 2 * A_ICLR_RANK
    r, k, v = p[..., :o1], p[..., o1:o2], p[..., o2:o3]
    wd = p[..., o3:o4].reshape(bsz, t_len, 2, A_DECAY_RANK)
    ad = p[..., o4:o5].reshape(bsz, t_len, 2, A_ICLR_RANK)
    gd = p[..., o5:]
    w = jnp.exp(-A_DECAY_SCALE * jax.nn.sigmoid(w0 + jnp.einsum('btdr,drc->btdc', jnp.tanh(wd), w2)))
    a = jax.nn.sigmoid(a0 + jnp.einsum('btdr,drc->btdc', ad, a2))
    g = jnp.einsum('btr,rc->btc', jax.nn.sigmoid(gd), g2)
    kk = heads_a(k * k_k)
    kk = kk * lax.rsqrt(jnp.sum(kk * kk, axis=-1, keepdims=True) + 1e-12)
    kd = heads_a(k[:, :, None, :] * (1 + (a - 1) * k_a))
    bd = kk[:, :, None] * heads_a(a)
    w_h = heads_a(w)
    r_h, v_h = heads_a(r), heads_a(v)
    s_f, y_f = rwkv7_scan(s0[0], r_h, w_h[:, :, 0], kd[:, :, 0], v_h, -kk, bd[:, :, 0], False)
    s_b, y_b = rwkv7_scan(s0[1], r_h, w_h[:, :, 1], kd[:, :, 1], v_h, -kk, bd[:, :, 1], True)
    y = y_f + y_b
    mean = jnp.mean(y, axis=-1, keepdims=True)
    var = jnp.mean(jnp.square(y - mean), axis=-1, keepdims=True)
    y = ((y - mean) * lax.rsqrt(var + A_LN_EPS)).reshape(bsz, t_len, A_WIDTH) * ln_w + ln_b
    bonus = jnp.sum(jnp.sum(r_h[:, :, None] * kd * r_k, axis=-1, keepdims=True) * v_h[:, :, None], axis=2)
    y = (y + bonus.reshape(bsz, t_len, A_WIDTH)) * g
    return y.astype(dt), jnp.stack([s_f, s_b])


def axial_rope_tables(seq_len):
    rows = seq_len // GRID_W
    row = jnp.repeat(jnp.arange(rows), GRID_W).astype(jnp.float32)
    col = jnp.tile(jnp.arange(GRID_W), rows).astype(jnp.float32)
    axis_dim = B_ROPE // 2
    inv = ROPE_BASE ** (-jnp.arange(0, axis_dim, 2, dtype=jnp.float32) / axis_dim)
    ang = jnp.concatenate([row[:, None] * inv, col[:, None] * inv], axis=-1)
    return jnp.cos(ang), jnp.sin(ang)


def apply_axial_rope(x, cos, sin):
    half = B_ROPE // 2
    xn, x1, x2 = x[..., :B_NOPE], x[..., B_NOPE:B_NOPE + half], x[..., B_NOPE + half:]
    c = cos[None, :, None, :].astype(x.dtype)
    s = sin[None, :, None, :].astype(x.dtype)
    return jnp.concatenate([xn, x1 * c - x2 * s, x2 * c + x1 * s], axis=-1)


def mla_qkv(p, rope, q_norm, w_q_up, kv_norm, w_kv_up, q_gain, k_gain):
    bsz, t_len, _ = p.shape
    pq = p[..., :B_Q_RANK]
    pkv = p[..., B_Q_RANK:B_Q_RANK + B_KV_RANK]
    pkr = p[..., B_Q_RANK + B_KV_RANK:]
    q = jnp.einsum('btr,rc->btc', rms_norm(pq, q_norm), w_q_up).reshape(bsz, t_len, B_HEADS, B_QK)
    kv = jnp.einsum('btr,rc->btc', rms_norm(pkv, kv_norm), w_kv_up).reshape(bsz, t_len, B_HEADS, B_NOPE + B_V)
    k = jnp.concatenate([kv[..., :B_NOPE], jnp.broadcast_to(pkr[:, :, None, :], (bsz, t_len, B_HEADS, B_ROPE))], axis=-1)
    v = kv[..., B_NOPE:]
    q, k = rms_norm(q, q_gain), rms_norm(k, k_gain)
    if rope is not None:
        q, k = apply_axial_rope(q, *rope), apply_axial_rope(k, *rope)
    return q, k, v


def attention(q, k, v):
    s = jnp.einsum('bqhd,bkhd->bhqk', q, k).astype(jnp.float32) * ATTN_SCALE
    pr = jax.nn.softmax(s, axis=-1).astype(v.dtype)
    return jnp.einsum('bhqk,bkhd->bqhd', pr, v)


def blocked_attention(q, k, v):
    bsz, t_len, nh, dq = q.shape
    nb = t_len // Q_BLOCK
    qb = jnp.moveaxis(q.reshape(bsz, nb, Q_BLOCK, nh, dq), 1, 0)
    out = lax.map(lambda qi: attention(qi, k, v), qb)
    return jnp.moveaxis(out, 0, 1).reshape(bsz, t_len, nh, v.shape[-1])


def even_mixer(h_lat, h_ctx, w_in, w_out, a_prm, b_prm, rope, need_ctx_out):
    bsz, seq, _ = h_lat.shape
    ctx_len = h_ctx.shape[1]
    p_lat = jnp.einsum('btd,dc->btc', h_lat, w_in)
    p_ctx = jnp.einsum('btd,dc->btc', h_ctx, w_in)
    s_zero = jnp.zeros((2, bsz, A_HEADS, A_HEAD_DIM, A_HEAD_DIM), jnp.float32)
    ya_ctx, s_ctx = rwkv7_mix(p_ctx[..., :A_COLS], s_zero, *a_prm)
    ya_lat, _ = rwkv7_mix(p_lat[..., :A_COLS], s_ctx, *a_prm)
    q_c, k_c, v_c = mla_qkv(p_ctx[..., A_COLS:], None, *b_prm)
    q_l, k_l, v_l = mla_qkv(p_lat[..., A_COLS:], rope, *b_prm)
    yb_lat = blocked_attention(q_l, jnp.concatenate([k_c, k_l], axis=1), jnp.concatenate([v_c, v_l], axis=1))
    o_lat = jnp.einsum('btc,cd->btd', jnp.concatenate([ya_lat, yb_lat.reshape(bsz, seq, B_WIDTH)], axis=-1), w_out)
    if not need_ctx_out:
        return o_lat, None
    yb_ctx = attention(q_c, k_c, v_c)
    o_ctx = jnp.einsum('btc,cd->btd', jnp.concatenate([ya_ctx, yb_ctx.reshape(bsz, ctx_len, B_WIDTH)], axis=-1), w_out)
    return o_lat, o_ctx


def fourier_mixer(h, w_out):
    bsz, t_len, d = h.shape
    hg = h.astype(jnp.float32).reshape(bsz, t_len, C_GROUPS, C_GROUP_DIM)
    f = jnp.fft.fft2(hg, axes=(1, 3), norm='ortho').real.reshape(bsz, t_len, d).astype(h.dtype)
    return jnp.einsum('btd,de->bte', f, w_out)


def swiglu(h, wg, wu, wd):
    return jnp.dot(jax.nn.silu(jnp.dot(h, wg)) * jnp.dot(h, wu), wd)


def moe_ffn(h, router_w, router_b, w_gate, w_up, w_down, sh_gate, sh_up, sh_down):
    n_tok, d = h.shape
    scores = jax.nn.sigmoid(jnp.dot(h.astype(jnp.float32), router_w.astype(jnp.float32)))
    sel = (scores + router_b.astype(jnp.float32)).reshape(n_tok, N_GROUPS, EXPERTS_PER_GROUP)
    group_score = jnp.sum(lax.top_k(sel, 2)[0], axis=-1)
    g_idx = jnp.argmax(group_score, axis=-1)
    sel_in = jnp.take_along_axis(sel, g_idx[:, None, None], axis=1)[:, 0]
    _, local = lax.top_k(sel_in, TOP_K)
    e_idx = g_idx[:, None] * EXPERTS_PER_GROUP + local
    wts = jnp.take_along_axis(scores, e_idx, axis=1)
    wts = wts / jnp.sum(wts, axis=-1, keepdims=True)
    n_asg = n_tok * TOP_K
    flat_e = e_idx.reshape(n_asg)
    flat_tok = jnp.repeat(jnp.arange(n_tok), TOP_K)
    order = jnp.argsort(flat_e)
    e_sorted, tok_sorted, w_sorted = flat_e[order], flat_tok[order], wts.reshape(n_asg)[order]
    counts = jnp.bincount(flat_e, length=N_EXPERTS)
    start = jnp.cumsum(counts) - counts
    padded = (counts + MOE_BLOCK - 1) // MOE_BLOCK * MOE_BLOCK
    pend = jnp.cumsum(padded)
    pstart = pend - padded
    dest = pstart[e_sorted] + (jnp.arange(n_asg) - start[e_sorted])
    n_blocks = (n_asg + N_EXPERTS * (MOE_BLOCK - 1) + MOE_BLOCK - 1) // MOE_BLOCK
    slot_tok = jnp.full((n_blocks * MOE_BLOCK,), n_tok, jnp.int32).at[dest].set(tok_sorted.astype(jnp.int32))
    block_e = jnp.minimum(jnp.searchsorted(pend, jnp.arange(n_blocks) * MOE_BLOCK, side='right'), N_EXPERTS - 1)
    h_pad = jnp.concatenate([h, jnp.zeros((1, d), h.dtype)], axis=0)
    xb = h_pad[slot_tok].reshape(n_blocks, MOE_BLOCK, d)
    yb = lax.map(lambda a: swiglu(a[0], w_gate[a[1]], w_up[a[1]], w_down[a[1]]), (xb, block_e))
    y_asg = yb.reshape(n_blocks * MOE_BLOCK, d)[dest] * w_sorted[:, None]
    routed = jax.ops.segment_sum(y_asg, tok_sorted, num_segments=n_tok)
    return (swiglu(h, sh_gate, sh_up, sh_down) + routed).astype(h.dtype)


def setup_inputs(seed: int = 0) -> dict:
    key = jax.random.key(seed)
    ks = iter(jax.random.split(key, 40))
    f32 = jnp.float32

    def nrm(shape, scale):
        return jax.random.normal(next(ks), shape, f32) * scale

    def gain(shape):
        return 1.0 + nrm(shape, 0.02)

    return {
        'x': nrm((BATCH, SEQ, D_MODEL), 1.0),
        'c': nrm((BATCH, D_MODEL), 1.0),
        'ctx': nrm((BATCH, CTX_LEN, D_MODEL), 1.0),
        'c_ctx': nrm((D_MODEL,), 1.0),
        'norm1_g': gain((DEPTH, D_MODEL)),
        'norm2_g': gain((DEPTH, D_MODEL)),
        'mod_w': nrm((DEPTH, D_MODEL, 6 * D_MODEL), 0.5 * D_MODEL ** -0.5),
        'mod_b': nrm((DEPTH, 6 * D_MODEL), 0.02),
        'w_in': nrm((N_EVEN, D_MODEL, IN_COLS), D_MODEL ** -0.5),
        'a_mu': jax.random.uniform(next(ks), (N_EVEN, A_COLS), f32),
        'a_w0': nrm((N_EVEN, 2, A_WIDTH), 1.5) - 1.0,
        'a_w2': nrm((N_EVEN, 2, A_DECAY_RANK, A_WIDTH), 0.5 * A_DECAY_RANK ** -0.5),
        'a_a0': nrm((N_EVEN, 2, A_WIDTH), 0.3),
        'a_a2': nrm((N_EVEN, 2, A_ICLR_RANK, A_WIDTH), 0.5 * A_ICLR_RANK ** -0.5),
        'a_g2': nrm((N_EVEN, A_GATE_RANK, A_WIDTH), A_GATE_RANK ** -0.5),
        'a_k_k': 0.85 + nrm((N_EVEN, A_WIDTH), 0.05),
        'a_k_a': 1.0 + nrm((N_EVEN, A_WIDTH), 0.05),
        'a_r_k': nrm((N_EVEN, A_HEADS, A_HEAD_DIM), 0.1),
        'a_ln_w': gain((N_EVEN, A_WIDTH)),
        'a_ln_b': nrm((N_EVEN, A_WIDTH), 0.02),
        'b_q_norm': gain((N_EVEN, B_Q_RANK)),
        'b_w_q_up': nrm((N_EVEN, B_Q_RANK, B_HEADS * B_QK), B_Q_RANK ** -0.5),
        'b_kv_norm': gain((N_EVEN, B_KV_RANK)),
        'b_w_kv_up': nrm((N_EVEN, B_KV_RANK, B_HEADS * (B_NOPE + B_V)), B_KV_RANK ** -0.5),
        'b_q_gain': gain((N_EVEN, B_QK)),
        'b_k_gain': gain((N_EVEN, B_QK)),
        'w_out_even': nrm((N_EVEN, MIX_WIDTH, D_MODEL), MIX_WIDTH ** -0.5),
        'w_out_fourier': nrm((N_ODD, D_MODEL, D_MODEL), D_MODEL ** -0.5),
        'router_w': nrm((D_MODEL, N_EXPERTS), D_MODEL ** -0.5),
        'router_b': nrm((N_EXPERTS,), 0.01),
        'moe_w_gate': nrm((DEPTH, N_EXPERTS, D_MODEL, D_EXPERT), D_MODEL ** -0.5),
        'moe_w_up': nrm((DEPTH, N_EXPERTS, D_MODEL, D_EXPERT), D_MODEL ** -0.5),
        'moe_w_down': nrm((DEPTH, N_EXPERTS, D_EXPERT, D_MODEL), D_EXPERT ** -0.5),
        'shared_w_gate': nrm((DEPTH, D_MODEL, D_SHARED), D_MODEL ** -0.5),
        'shared_w_up': nrm((DEPTH, D_MODEL, D_SHARED), D_MODEL ** -0.5),
        'shared_w_down': nrm((DEPTH, D_SHARED, D_MODEL), D_SHARED ** -0.5),
    }


def reference(x, c, ctx, c_ctx, norm1_g, norm2_g, mod_w, mod_b, w_in, a_mu, a_w0, a_w2, a_a0, a_a2, a_g2,
              a_k_k, a_k_a, a_r_k, a_ln_w, a_ln_b, b_q_norm, b_w_q_up, b_kv_norm, b_w_kv_up, b_q_gain, b_k_gain,
              w_out_even, w_out_fourier, router_w, router_b, moe_w_gate, moe_w_up, moe_w_down,
              shared_w_gate, shared_w_up, shared_w_down):
    bsz, seq, d = x.shape
    ctx_len = ctx.shape[1]
    rope = axial_rope_tables(seq)
    for i in range(DEPTH):
        last = i == DEPTH - 1
        j = i // 2
        sh1, sc1, g1, sh2, sc2, g2 = jnp.split((jnp.dot(jax.nn.silu(c), mod_w[i]) + mod_b[i])[:, None, :], 6, axis=-1)
        sh1c, sc1c, g1c, sh2c, sc2c, g2c = jnp.split(jnp.dot(jax.nn.silu(c_ctx), mod_w[i]) + mod_b[i], 6, axis=-1)
        h_l = modulate(x, norm1_g[i], sh1, sc1)
        h_c = modulate(ctx, norm1_g[i], sh1c, sc1c)
        if i % 2 == 0:
            a_prm = (a_mu[j], a_w0[j], a_w2[j], a_a0[j], a_a2[j], a_g2[j], a_k_k[j], a_k_a[j], a_r_k[j], a_ln_w[j], a_ln_b[j])
            b_prm = (b_q_norm[j], b_w_q_up[j], b_kv_norm[j], b_w_kv_up[j], b_q_gain[j], b_k_gain[j])
            o_l, o_c = even_mixer(h_l, h_c, w_in[j], w_out_even[j], a_prm, b_prm, rope, not last)
        else:
            o_l = fourier_mixer(h_l, w_out_fourier[j])
            o_c = None if last else fourier_mixer(h_c, w_out_fourier[j])
        x = x + g1 * o_l
        moe_prm = (router_w, router_b, moe_w_gate[i], moe_w_up[i], moe_w_down[i],
                   shared_w_gate[i], shared_w_up[i], shared_w_down[i])
        h2_l = modulate(x, norm2_g[i], sh2, sc2).reshape(bsz * seq, d)
        if last:
            x = x + g2 * moe_ffn(h2_l, *moe_prm).reshape(bsz, seq, d)
        else:
            ctx = ctx + g1c * o_c
            h2_c = modulate(ctx, norm2_g[i], sh2c, sc2c).reshape(bsz * ctx_len, d)
            y = moe_ffn(jnp.concatenate([h2_l, h2_c], axis=0), *moe_prm)
            x = x + g2 * y[:bsz * seq].reshape(bsz, seq, d)
            ctx = ctx + g2c * y[bsz * seq:].reshape(bsz, ctx_len, d)
    return x
```

```python
import functools

import numpy as np
import jax
import jax.numpy as jnp
from jax import lax
from jax.experimental import pallas as pl
from jax.experimental.pallas import tpu as pltpu

F32 = jnp.float32
BF16 = jnp.bfloat16
HI = lax.Precision.HIGHEST

D = 1024
DEPTH = 4
GRID_W = 64
NORM_EPS = 1e-6
A_HEAD = 64
A_HEADS = 8
A_W = 512
A_RANK = 64
A_GATE = 128
A_DECAY_SCALE = 0.606531
A_LN_EPS = 64e-5
A_COLS = 3 * A_W + 4 * A_RANK + A_GATE
B_HEADS = 8
B_NOPE = 64
B_ROPE = 32
B_QK = 96
B_V = 64
B_Q_RANK = 256
B_KV_RANK = 128
B_COLS = B_Q_RANK + B_KV_RANK + B_ROPE
ATTN_SCALE = B_QK ** -0.5
ROPE_BASE = 10000.0
C_GROUPS = 8
C_GDIM = 128
N_EXPERTS = 16
N_GROUPS = 4
EPG = 4
D_EXPERT = 512
D_SHARED = 512

LANE = 128
TM = 256
CH = 64
MB = 256
PB_COLS = 512
HP = 128
N_CLASS = 24
CLS_ROWS = 32
EXT = 128
PAIRS = ((0, 1), (0, 2), (0, 3), (1, 2), (1, 3), (2, 3))
VMEM_LIMIT = 56 * 1024 * 1024


def _cparams(sem):
    return pltpu.CompilerParams(dimension_semantics=sem, vmem_limit_bytes=VMEM_LIMIT)


def _mm(a, b):
    return jnp.dot(a.astype(BF16), b.astype(BF16), preferred_element_type=F32)


def _mm_nt(a, b):
    return lax.dot_general(a.astype(BF16), b.astype(BF16), (((1,), (1,)), ((), ())),
                           preferred_element_type=F32)


def _mm_tn(a, b):
    return lax.dot_general(a.astype(BF16), b.astype(BF16), (((0,), (0,)), ((), ())),
                           preferred_element_type=F32)


def _mm_hi(a, b):
    return jnp.dot(a, b, precision=HI, preferred_element_type=F32)


def _sigmoid(x):
    return 1.0 / (1.0 + jnp.exp(-x))


def _silu(x):
    return x * _sigmoid(x)


def _rms_mod(x, scale, shift):
    ms = jnp.mean(x * x, axis=-1, keepdims=True)
    return x * lax.rsqrt(ms + NORM_EPS) * scale + shift


def _mod_kernel(c_ref, w_ref, b_ref, o_ref):
    o_ref[...] = _mm_hi(_silu(c_ref[...]), w_ref[...]) + b_ref[...]


def _modulation(cs, mod_w, mod_b):
    rows = cs.shape[0]
    tn = 1536
    return pl.pallas_call(
        _mod_kernel,
        grid=(DEPTH, 6 * D // tn),
        in_specs=[pl.BlockSpec((rows, D), lambda l, n: (0, 0)),
                  pl.BlockSpec((None, D, tn), lambda l, n: (l, 0, n)),
                  pl.BlockSpec((None, 1, tn), lambda l, n: (l, 0, n))],
        out_specs=pl.BlockSpec((None, rows, tn), lambda l, n: (l, 0, n)),
        out_shape=jax.ShapeDtypeStruct((DEPTH, rows, 6 * D), F32),
        compiler_params=_cparams(("arbitrary", "arbitrary")),
        name="modulation",
    )(cs, mod_w, mod_b.reshape(DEPTH, 1, 6 * D))


def _tab_spec(nct):
    return pl.BlockSpec((None, None, 1, D), lambda b, t: (b, jnp.minimum(t // nct, 1), 0, 0))


def _e1_kernel(x_ref, sc_ref, sh_ref, w_ref, pa_ref, pb_ref):
    h = _rms_mod(x_ref[...], sc_ref[...], sh_ref[...])
    p = _mm(h, w_ref[...])
    pa_ref[...] = p[:, :A_COLS]
    pb_ref[...] = p[:, A_COLS:]


def _in_proj(x, sc, sh, w, nct):
    bsz, s, _ = x.shape
    ncol = A_COLS + PB_COLS
    return pl.pallas_call(
        _e1_kernel,
        grid=(bsz, s // TM),
        in_specs=[pl.BlockSpec((None, TM, D), lambda b, t: (b, t, 0)),
                  _tab_spec(nct), _tab_spec(nct),
                  pl.BlockSpec((D, ncol), lambda b, t: (0, 0))],
        out_specs=[pl.BlockSpec((None, TM, A_COLS), lambda b, t: (b, t, 0)),
                   pl.BlockSpec((None, TM, PB_COLS), lambda b, t: (b, t, 0))],
        out_shape=[jax.ShapeDtypeStruct((bsz, s, A_COLS), F32),
                   jax.ShapeDtypeStruct((bsz, s, PB_COLS), F32)],
        compiler_params=_cparams(("parallel", "parallel")),
        name="in_proj",
    )(x, sc, sh, w)


def _chunk_of(i, nctx, nch, reverse):
    if not reverse:
        return i
    return jnp.where(i < nctx, nctx - 1 - i, nch - 1 - (i - nctx))


def _rw_kernel(*refs, reverse, final, nctx, nch):
    if final:
        (pc_ref, pp_ref, pn_ref, mu_ref, w0_ref, w2_ref, a0_ref, a2_ref, kk_ref, ka_ref,
         tri_ref, ms_ref, mi_ref, bones_ref,
         a0o_ref, a2o_ref, g2_ref, rk_ref, lnw_ref, lnb_ref, yf_ref, out_ref, z_ref) = refs
    else:
        (pc_ref, pp_ref, pn_ref, mu_ref, w0_ref, w2_ref, a0_ref, a2_ref, kk_ref, ka_ref,
         tri_ref, ms_ref, mi_ref, bones_ref, out_ref, z_ref) = refs
    i = pl.program_id(1)
    c = _chunk_of(i, nctx, nch, reverse)

    @pl.when(i == 0)
    def _():
        z_ref[...] = jnp.zeros_like(z_ref)

    p = pc_ref[...]
    seg_first = jnp.logical_or(c == 0, c == nctx)
    seg_last = jnp.logical_or(c == nctx - 1, c == nch - 1)
    prow = jnp.where(seg_first, 0.0, pp_ref[7:8, :])
    nrow = jnp.where(seg_last, 0.0, pn_ref[0:1, :])
    rid = lax.broadcasted_iota(jnp.int32, (CH, 1), 0)
    pprev = jnp.where(rid == 0, prow, pltpu.roll(p, 1, 0))
    pnext = jnp.where(rid == CH - 1, nrow, pltpu.roll(p, CH - 1, 0))
    ps = p + mu_ref[...] * (0.5 * (pprev + pnext) - p)

    r = ps[:, 0:A_W]
    k = ps[:, A_W:2 * A_W]
    v = ps[:, 2 * A_W:3 * A_W]
    dd = ps[:, 3 * A_W:3 * A_W + 128]
    ad = ps[:, 3 * A_W + 128:3 * A_W + 256]
    gd = ps[:, 3 * A_W + 256:3 * A_W + 384]

    logw = -A_DECAY_SCALE * _sigmoid(w0_ref[...] + _mm(jnp.tanh(dd), w2_ref[...]))
    aic = _sigmoid(a0_ref[...] + _mm(ad, a2_ref[...]))
    bones = bones_ref[...]
    kkr = k * kk_ref[...]
    kk = kkr * lax.rsqrt(_mm_hi(kkr * kkr, bones) + 1e-12)
    kd = k * (1.0 + (aic - 1.0) * ka_ref[...])
    bd = kk * aic
    av = -kk

    cum = _mm_hi(tri_ref[...], logw)
    tot = cum[0:1, :] if reverse else cum[CH - 1:CH, :]
    rt = r * jnp.exp(cum)
    at = av * jnp.exp(cum - logw)
    ei = jnp.exp(-cum)
    kt = kd * ei
    bt = bd * ei
    et = jnp.exp(tot - cum)
    kh = kd * et
    bh = bd * et
    gam = jnp.exp(tot)

    lane = lax.broadcasted_iota(jnp.int32, (1, LANE), 1)
    m0 = (lane < A_HEAD).astype(F32)
    m1 = 1.0 - m0
    eye = (lax.broadcasted_iota(jnp.int32, (LANE, LANE), 0)
           == lax.broadcasted_iota(jnp.int32, (LANE, LANE), 1))
    mstrict = ms_ref[...] > 0.5
    mincl = mi_ref[...] > 0.5

    def stack(x):
        return jnp.concatenate([x * m0, x * m1], axis=0)

    ys = []
    for j in range(A_HEADS // 2):
        sl = slice(LANE * j, LANE * (j + 1))
        ats, rts, bts, kts = stack(at[:, sl]), stack(rt[:, sl]), stack(bt[:, sl]), stack(kt[:, sl])
        vs, bhs, khs = stack(v[:, sl]), stack(bh[:, sl]), stack(kh[:, sl])
        big = _mm_nt(jnp.concatenate([ats, rts], axis=0), jnp.concatenate([bts, kts], axis=0))
        aab = jnp.where(mstrict, big[0:128, 0:128], 0.0)
        aak = jnp.where(mstrict, big[0:128, 128:256], 0.0)
        arb = jnp.where(mincl, big[128:256, 0:128], 0.0)
        ark = jnp.where(mincl, big[128:256, 128:256], 0.0)
        avs = _mm(jnp.concatenate([aak, ark], axis=0), vs)
        x = jnp.concatenate([ats, avs[0:128]], axis=1)
        nm = aab
        for it in range(6):
            x = x + _mm(nm, x)
            if it < 5:
                nm = _mm(nm, nm)
        qy = jnp.concatenate([rts, avs[128:256]], axis=1) + _mm(arb, x)
        qp = qy[0:CH, 0:128] + qy[CH:2 * CH, 0:128]
        y0 = qy[0:CH, 128:256] + qy[CH:2 * CH, 128:256]
        z = z_ref[j]
        ys.append(y0 + _mm(qp, z))
        lhs = jnp.concatenate([bhs, khs], axis=0)
        rhs = jnp.concatenate(
            [x, jnp.concatenate([jnp.zeros((128, 128), F32), vs], axis=1)], axis=0)
        m = _mm_tn(lhs, rhs)
        gt = jnp.where(eye, gam[:, sl], 0.0) + m[:, 0:128]
        z_ref[j] = _mm(gt, z) + m[:, 128:256]
    y = jnp.concatenate(ys, axis=1)

    if not final:
        out_ref[...] = y
    else:
        y = y + yf_ref[...]
        mean = _mm_hi(y, bones) * (1.0 / A_HEAD)
        yc = y - mean
        var = _mm_hi(yc * yc, bones) * (1.0 / A_HEAD)
        yn = yc * lax.rsqrt(var + A_LN_EPS) * lnw_ref[...] + lnb_ref[...]
        aico = _sigmoid(a0o_ref[...] + _mm(ad, a2o_ref[...]))
        kdo = k * (1.0 + (aico - 1.0) * ka_ref[...])
        bonus = _mm_hi(r * rk_ref[...] * (kd + kdo), bones) * v
        g = _mm(_sigmoid(gd), g2_ref[...])
        out_ref[...] = ((yn + bonus) * g).astype(out_ref.dtype)


def _rwkv_pass(pa, prm, consts, yf, *, reverse, nctx):
    bsz, s, _ = pa.shape
    nch = s // CH
    final = yf is not None
    d = 1 if reverse else 0

    def cmap(b, i):
        return (b, _chunk_of(i, nctx, nch, reverse), 0)

    def pmap(b, i):
        return (b, jnp.maximum(_chunk_of(i, nctx, nch, reverse) * (CH // 8) - 1, 0), 0)

    def nmap(b, i):
        return (b, jnp.minimum((_chunk_of(i, nctx, nch, reverse) + 1) * (CH // 8), s // 8 - 1), 0)

    def full(a):
        return pl.BlockSpec(a.shape, lambda b, i: (0,) * a.ndim)

    args = [pa, pa, pa, prm["mu"], prm["w0"][d], prm["w2"][d], prm["a0"][d], prm["a2"][d],
            prm["kk"], prm["ka"], consts["tri"][d], consts["mstrict"][d], consts["mincl"][d],
            consts["bones"]]
    specs = [pl.BlockSpec((None, CH, A_COLS), cmap),
             pl.BlockSpec((None, 8, A_COLS), pmap),
             pl.BlockSpec((None, 8, A_COLS), nmap)] + [full(a) for a in args[3:]]
    if final:
        extra = [prm["a0"][1 - d], prm["a2"][1 - d], prm["g2"], prm["rk"], prm["lnw"], prm["lnb"]]
        args += extra + [yf]
        specs += [full(a) for a in extra] + [pl.BlockSpec((None, CH, A_W), cmap)]
    return pl.pallas_call(
        functools.partial(_rw_kernel, reverse=reverse, final=final, nctx=nctx, nch=nch),
        grid=(bsz, nch),
        in_specs=specs,
        out_specs=pl.BlockSpec((None, CH, A_W), cmap),
        out_shape=jax.ShapeDtypeStruct((bsz, s, A_W), BF16 if final else F32),
        scratch_shapes=[pltpu.VMEM((A_HEADS // 2, LANE, LANE), F32)],
        compiler_params=_cparams(("parallel", "arbitrary")),
        name="rwkv_bwd" if reverse else "rwkv_fwd",
    )(*args)


def _mla_prep_kernel(pb_ref, qn_ref, kvn_ref, wq_ref, wk_ref, qg_ref, kg_ref,
                     c_ref, s1_ref, s2_ref, q_ref, k_ref, v_ref):
    pb = pb_ref[...]
    pq = pb[:, 0:B_Q_RANK]
    xq = pq * lax.rsqrt(jnp.mean(pq * pq, axis=-1, keepdims=True) + NORM_EPS) * qn_ref[...]
    q = _mm(xq, wq_ref[...])
    pk = pb[:, B_Q_RANK:PB_COLS]
    is_kv = lax.broadcasted_iota(jnp.int32, (1, PB_COLS - B_Q_RANK), 1) < B_KV_RANK
    ms = jnp.sum(jnp.where(is_kv, pk * pk, 0.0), axis=-1, keepdims=True) * (1.0 / B_KV_RANK)
    xk = jnp.where(is_kv, pk * lax.rsqrt(ms + NORM_EPS) * kvn_ref[...], pk)
    kv = _mm(xk, wk_ref[...])
    cc, s1, s2 = c_ref[...], s1_ref[...], s2_ref[...]

    def head(xh, gain):
        ss = jnp.sum(xh * xh, axis=-1, keepdims=True) * (1.0 / B_QK)
        xh = xh * lax.rsqrt(ss + NORM_EPS) * gain
        return xh * cc + pltpu.roll(xh, B_ROPE // 2, 1) * s1 + pltpu.roll(xh, HP - B_ROPE // 2, 1) * s2

    for h in range(B_HEADS):
        sl = slice(HP * h, HP * (h + 1))
        q_ref[:, sl] = (head(q[:, sl], qg_ref[...]) * ATTN_SCALE).astype(BF16)
        k_ref[:, sl] = head(kv[:, sl], kg_ref[...]).astype(BF16)
    v_ref[...] = kv[:, B_HEADS * HP:].astype(BF16)


def _mla_prep(pb, prm, rope):
    bsz, s, _ = pb.shape

    def full(a):
        return pl.BlockSpec(a.shape, lambda b, t: (0,) * a.ndim)

    consts = [prm["qn"], prm["kvn"], prm["wq"], prm["wk"], prm["qg"], prm["kg"]]
    rspec = pl.BlockSpec((TM, HP), lambda b, t: (t, 0))
    return pl.pallas_call(
        _mla_prep_kernel,
        grid=(bsz, s // TM),
        in_specs=[pl.BlockSpec((None, TM, PB_COLS), lambda b, t: (b, t, 0))]
        + [full(a) for a in consts] + [rspec, rspec, rspec],
        out_specs=[pl.BlockSpec((None, TM, B_HEADS * HP), lambda b, t: (b, t, 0)),
                   pl.BlockSpec((None, TM, B_HEADS * HP), lambda b, t: (b, t, 0)),
                   pl.BlockSpec((None, TM, B_HEADS * B_V), lambda b, t: (b, t, 0))],
        out_shape=[jax.ShapeDtypeStruct((bsz, s, B_HEADS * HP), BF16),
                   jax.ShapeDtypeStruct((bsz, s, B_HEADS * HP), BF16),
                   jax.ShapeDtypeStruct((bsz, s, B_HEADS * B_V), BF16)],
        compiler_params=_cparams(("parallel", "parallel")),
        name="mla_prep",
    )(pb, *consts, *rope)


def _att_kernel(q_ref, k_ref, v_ref, o_ref, *, ctx_len, nct):
    t = pl.program_id(2)
    lane = lax.broadcasted_iota(jnp.int32, (1, 2 * B_V), 1)

    def run(nk):
        acc = jnp.zeros((TM, 2 * B_V), F32)
        for s in range(2):
            q = q_ref[:, HP * s:HP * (s + 1)]
            k = k_ref[0:nk, HP * s:HP * (s + 1)]
            sc = lax.dot_general(q, k, (((1,), (1,)), ((), ())), preferred_element_type=F32)
            p = jnp.exp(sc - jnp.max(sc, axis=-1, keepdims=True))
            l = jnp.sum(p, axis=-1, keepdims=True)
            vm = jnp.where((lane >= B_V * s) & (lane < B_V * (s + 1)), v_ref[0:nk, :], 0)
            acc = acc + jnp.dot(p.astype(BF16), vm.astype(BF16), preferred_element_type=F32) / l
        o_ref[...] = acc.astype(BF16)

    @pl.when(t < nct)
    def _():
        run(ctx_len)

    @pl.when(t >= nct)
    def _():
        run(k_ref.shape[0])


def _attention(q, k, v, ctx_len):
    bsz, s, _ = q.shape
    nct = ctx_len // TM
    return pl.pallas_call(
        functools.partial(_att_kernel, ctx_len=ctx_len, nct=nct),
        grid=(bsz, B_HEADS // 2, s // TM),
        in_specs=[pl.BlockSpec((None, TM, 2 * HP), lambda b, j, t: (b, t, j)),
                  pl.BlockSpec((None, s, 2 * HP), lambda b, j, t: (b, 0, j)),
                  pl.BlockSpec((None, s, 2 * B_V), lambda b, j, t: (b, 0, j))],
        out_specs=pl.BlockSpec((None, TM, 2 * B_V), lambda b, j, t: (b, t, j)),
        out_shape=jax.ShapeDtypeStruct((bsz, s, B_HEADS * B_V), BF16),
        compiler_params=_cparams(("parallel", "parallel", "parallel")),
        name="mla_attention",
    )(q, k, v)


def _route_tail(x, o, g1, sc2, sh2, rwt_ref, rb_ref, striu_ref, xo_ref, he_ref, rt_ref, cnt_ref,
                cnt_scr, first):
    xn = x + g1 * o
    xo_ref[...] = xn
    h2 = _rms_mod(xn, sc2, sh2)
    he_ref[:, 0:D] = h2

    @pl.when(first)
    def _():
        cnt_scr[...] = jnp.zeros_like(cnt_scr)

    logits = lax.dot_general(rwt_ref[...], h2, (((1,), (1,)), ((), ())), precision=HI,
                             preferred_element_type=F32)
    scores = _sigmoid(logits)
    sel = scores + rb_ref[...]
    srow = [scores[e:e + 1, :] for e in range(N_EXPERTS)]
    xrow = [sel[e:e + 1, :] for e in range(N_EXPERTS)]
    best_g = None
    for g in range(N_GROUPS):
        best_p = None
        for idx, (a, b) in enumerate(PAIRS):
            psum = xrow[EPG * g + a] + xrow[EPG * g + b]
            if best_p is None:
                best_p, pid = psum, jnp.zeros_like(psum, dtype=jnp.int32)
            else:
                better = psum > best_p
                best_p = jnp.where(better, psum, best_p)
                pid = jnp.where(better, idx, pid)
        if best_g is None:
            best_g, cls = best_p, pid
        else:
            better = best_p > best_g
            best_g = jnp.where(better, best_p, best_g)
            cls = jnp.where(better, len(PAIRS) * g + pid, cls)
    sa = jnp.zeros_like(best_g)
    sb = jnp.zeros_like(best_g)
    for cidx in range(N_CLASS):
        g, (a, b) = cidx // len(PAIRS), PAIRS[cidx % len(PAIRS)]
        hit = cls == cidx
        sa = jnp.where(hit, srow[EPG * g + a], sa)
        sb = jnp.where(hit, srow[EPG * g + b], sb)
    den = sa + sb
    wa = sa / den
    wb = sb / den

    crow = lax.broadcasted_iota(jnp.int32, (CLS_ROWS, TM), 0)
    onehot = (crow == cls).astype(F32)
    prefix = _mm(onehot, striu_ref[...])
    run = cnt_scr[:, 0:1]
    rank = jnp.sum(onehot * (prefix + run), axis=0, keepdims=True)
    cnt_scr[...] = cnt_scr[...] + jnp.sum(onehot, axis=1, keepdims=True)
    cnt_ref[...] = cnt_scr[...]

    r8 = lax.broadcasted_iota(jnp.int32, (8, TM), 0)
    rt_ref[...] = jnp.where(r8 == 0, cls.astype(F32), jnp.where(r8 == 1, rank, 0.0))
    rr = lax.broadcasted_iota(jnp.int32, (EXT, TM), 0)
    wmat = jnp.where(rr == 0, wa, jnp.where(rr == 1, wb, 0.0))
    eye = (lax.broadcasted_iota(jnp.int32, (TM, TM), 0)
           == lax.broadcasted_iota(jnp.int32, (TM, TM), 1)).astype(F32)
    he_ref[:, D:D + EXT] = lax.dot_general(eye, wmat, (((1,), (1,)), ((), ())), precision=HI,
                                           preferred_element_type=F32)


def _tail_out(bsz, s):
    nt = s // TM
    specs = [pl.BlockSpec((None, TM, D), lambda b, t: (b, t, 0)),
             pl.BlockSpec((None, TM, D + EXT), lambda b, t: (b, t, 0)),
             pl.BlockSpec((None, 8, TM), lambda b, t: (b * nt + t, 0, 0)),
             pl.BlockSpec((CLS_ROWS, LANE), lambda b, t: (0, 0))]
    shapes = [jax.ShapeDtypeStruct((bsz, s, D), F32),
              jax.ShapeDtypeStruct((bsz, s, D + EXT), F32),
              jax.ShapeDtypeStruct((bsz * nt, 8, TM), F32),
              jax.ShapeDtypeStruct((CLS_ROWS, LANE), F32)]
    return specs, shapes


def _e2_kernel(ya_ref, yb_ref, wo_ref, x_ref, g1_ref, sc_ref, sh_ref, rwt_ref, rb_ref, striu_ref,
               xo_ref, he_ref, rt_ref, cnt_ref, cnt_scr):
    o = (jnp.dot(ya_ref[...], wo_ref[0:A_W, :], preferred_element_type=F32)
         + jnp.dot(yb_ref[...], wo_ref[A_W:2 * A_W, :], preferred_element_type=F32))
    first = jnp.logical_and(pl.program_id(0) == 0, pl.program_id(1) == 0)
    _route_tail(x_ref[...], o, g1_ref[...], sc_ref[...], sh_ref[...], rwt_ref, rb_ref, striu_ref,
                xo_ref, he_ref, rt_ref, cnt_ref, cnt_scr, first)


def _even_out(ya, yb, wo, x, g1, sc2, sh2, rwt, rb, striu, nct):
    bsz, s, _ = x.shape
    out_specs, out_shape = _tail_out(bsz, s)

    def full(a):
        return pl.BlockSpec(a.shape, lambda b, t: (0,) * a.ndim)

    return pl.pallas_call(
        _e2_kernel,
        grid=(bsz, s // TM),
        in_specs=[pl.BlockSpec((None, TM, A_W), lambda b, t: (b, t, 0)),
                  pl.BlockSpec((None, TM, A_W), lambda b, t: (b, t, 0)),
                  full(wo),
                  pl.BlockSpec((None, TM, D), lambda b, t: (b, t, 0)),
                  _tab_spec(nct), _tab_spec(nct), _tab_spec(nct),
                  full(rwt), full(rb), full(striu)],
        out_specs=out_specs,
        out_shape=out_shape,
        scratch_shapes=[pltpu.VMEM((CLS_ROWS, LANE), F32)],
        compiler_params=_cparams(("arbitrary", "arbitrary")),
        name="even_out_route",
    )(ya, yb, wo, x, g1, sc2, sh2, rwt, rb, striu)


def _f1_kernel(x_ref, sc_ref, sh_ref, cs_ref, hc_ref, hs_ref):
    h = _rms_mod(x_ref[...], sc_ref[...], sh_ref[...]).astype(BF16)
    for g in range(C_GROUPS):
        sl = slice(C_GDIM * g, C_GDIM * (g + 1))
        r = jnp.dot(h[:, sl], cs_ref[...], preferred_element_type=F32)
        hc_ref[:, sl] = r[:, 0:C_GDIM].astype(BF16)
        hs_ref[:, sl] = r[:, C_GDIM:2 * C_GDIM].astype(BF16)


def _fourier_channels(x, sc, sh, cs, nct):
    bsz, s, _ = x.shape
    return pl.pallas_call(
        _f1_kernel,
        grid=(bsz, s // TM),
        in_specs=[pl.BlockSpec((None, TM, D), lambda b, t: (b, t, 0)),
                  _tab_spec(nct), _tab_spec(nct),
                  pl.BlockSpec(cs.shape, lambda b, t: (0, 0))],
        out_specs=[pl.BlockSpec((None, TM, D), lambda b, t: (b, t, 0)),
                   pl.BlockSpec((None, TM, D), lambda b, t: (b, t, 0))],
        out_shape=[jax.ShapeDtypeStruct((bsz, s, D), BF16),
                   jax.ShapeDtypeStruct((bsz, s, D), BF16)],
        compiler_params=_cparams(("parallel", "parallel")),
        name="fourier_channels",
    )(x, sc, sh, cs)


def _f2_kernel(hc_ref, hs_ref, dl_ref, dc_ref, wo_ref, x_ref, g1_ref, sc_ref, sh_ref, rwt_ref, rb_ref,
               striu_ref, xo_ref, he_ref, rt_ref, cnt_ref, cnt_scr, f_scr, *, ctx_len, nct):
    t = pl.program_id(1)
    s = hc_ref.shape[0]
    seq = s - ctx_len

    @pl.when(t < nct)
    def _():
        f_scr[...] = (jnp.dot(dc_ref[:, 0:ctx_len], hc_ref[0:ctx_len, :], preferred_element_type=F32)
                      + jnp.dot(dc_ref[:, ctx_len:2 * ctx_len], hs_ref[0:ctx_len, :],
                                preferred_element_type=F32))

    @pl.when(t >= nct)
    def _():
        f_scr[...] = (jnp.dot(dl_ref[:, 0:seq], hc_ref[ctx_len:s, :], preferred_element_type=F32)
                      + jnp.dot(dl_ref[:, seq:2 * seq], hs_ref[ctx_len:s, :],
                                preferred_element_type=F32))

    o = _mm(f_scr[...], wo_ref[...])
    first = jnp.logical_and(pl.program_id(0) == 0, t == 0)
    _route_tail(x_ref[...], o, g1_ref[...], sc_ref[...], sh_ref[...], rwt_ref, rb_ref, striu_ref,
                xo_ref, he_ref, rt_ref, cnt_ref, cnt_scr, first)


def _fourier_out(hc, hs, dlat, dctx, wo, x, g1, sc2, sh2, rwt, rb, striu, ctx_len):
    bsz, s, _ = x.shape
    nct = ctx_len // TM
    out_specs, out_shape = _tail_out(bsz, s)

    def full(a):
        return pl.BlockSpec(a.shape, lambda b, t: (0,) * a.ndim)

    return pl.pallas_call(
        functools.partial(_f2_kernel, ctx_len=ctx_len, nct=nct),
        grid=(bsz, s // TM),
        in_specs=[pl.BlockSpec((None, s, D), lambda b, t: (b, 0, 0)),
                  pl.BlockSpec((None, s, D), lambda b, t: (b, 0, 0)),
                  pl.BlockSpec((TM, dlat.shape[1]), lambda b, t: (jnp.maximum(t - nct, 0), 0)),
                  pl.BlockSpec((TM, dctx.shape[1]), lambda b, t: (jnp.minimum(t, nct - 1), 0)),
                  full(wo),
                  pl.BlockSpec((None, TM, D), lambda b, t: (b, t, 0)),
                  _tab_spec(nct), _tab_spec(nct), _tab_spec(nct),
                  full(rwt), full(rb), full(striu)],
        out_specs=out_specs,
        out_shape=out_shape,
        scratch_shapes=[pltpu.VMEM((CLS_ROWS, LANE), F32), pltpu.VMEM((TM, D), F32)],
        compiler_params=_cparams(("arbitrary", "arbitrary")),
        name="fourier_out_route",
    )(hc, hs, dlat, dctx, wo, x, g1, sc2, sh2, rwt, rb, striu)


def _row_copy(src_ref, src_row, dst_ref, dst_row, sem):
    return pltpu.make_async_copy(src_ref.at[pl.ds(src_row, 1)], dst_ref.at[pl.ds(dst_row, 1)], sem)


def _disp_kernel(slot_ref, h_ref, xin_ref, xs_ref, sem):
    del xin_ref
    base = pl.program_id(0) * TM

    def issue(r, carry):
        _row_copy(h_ref, r, xs_ref, slot_ref[base + r], sem).start()
        return carry

    lax.fori_loop(0, TM, issue, 0)

    def drain(r, carry):
        _row_copy(h_ref, 0, xs_ref, 0, sem).wait()
        return carry

    lax.fori_loop(0, TM, drain, 0)


def _dispatch(slot, he, n_slots):
    n = he.shape[0]
    xs0 = jnp.zeros((n_slots, D + EXT), F32)
    return pl.pallas_call(
        _disp_kernel,
        grid_spec=pltpu.PrefetchScalarGridSpec(
            num_scalar_prefetch=1,
            grid=(n // TM,),
            in_specs=[pl.BlockSpec((TM, D + EXT), lambda i, sl: (i, 0)),
                      pl.BlockSpec(memory_space=pl.ANY)],
            out_specs=pl.BlockSpec(memory_space=pl.ANY),
            scratch_shapes=[pltpu.SemaphoreType.DMA(())]),
        out_shape=jax.ShapeDtypeStruct((n_slots, D + EXT), F32),
        input_output_aliases={2: 0},
        compiler_params=_cparams(("arbitrary",)),
        name="moe_dispatch",
    )(slot, he, xs0)


def _moe_kernel(ea_ref, eb_ref, ok_ref, xs_ref, wga_ref, wgb_ref, wua_ref, wub_ref, wda_ref, wdb_ref,
                y_ref):
    i = pl.program_id(0)

    @pl.when(ok_ref[i] > 0)
    def _():
        xe = xs_ref[...]
        x = xe[:, 0:D].astype(BF16)
        wa = xe[:, D:D + 1]
        wb = xe[:, D + 1:D + 2]
        ha = _silu(jnp.dot(x, wga_ref[...], preferred_element_type=F32)) \
            * jnp.dot(x, wua_ref[...], preferred_element_type=F32) * wa
        hb = _silu(jnp.dot(x, wgb_ref[...], preferred_element_type=F32)) \
            * jnp.dot(x, wub_ref[...], preferred_element_type=F32) * wb
        y_ref[...] = (jnp.dot(ha.astype(BF16), wda_ref[...], preferred_element_type=F32)
                      + jnp.dot(hb.astype(BF16), wdb_ref[...], preferred_element_type=F32))

    @pl.when(ok_ref[i] == 0)
    def _():
        y_ref[...] = jnp.zeros_like(y_ref)


def _moe_blocks(ea, eb, ok, xs, wg, wu, wd):
    n_slots = xs.shape[0]
    wa_spec = pl.BlockSpec((None, D, D_EXPERT), lambda i, ea, eb, ok: (ea[i], 0, 0))
    wb_spec = pl.BlockSpec((None, D, D_EXPERT), lambda i, ea, eb, ok: (eb[i], 0, 0))
    da_spec = pl.BlockSpec((None, D_EXPERT, D), lambda i, ea, eb, ok: (ea[i], 0, 0))
    db_spec = pl.BlockSpec((None, D_EXPERT, D), lambda i, ea, eb, ok: (eb[i], 0, 0))
    return pl.pallas_call(
        _moe_kernel,
        grid_spec=pltpu.PrefetchScalarGridSpec(
            num_scalar_prefetch=3,
            grid=(n_slots // MB,),
            in_specs=[pl.BlockSpec((MB, D + EXT), lambda i, ea, eb, ok: (i, 0)),
                      wa_spec, wb_spec, wa_spec, wb_spec, da_spec, db_spec],
            out_specs=pl.BlockSpec((MB, D), lambda i, ea, eb, ok: (i, 0))),
        out_shape=jax.ShapeDtypeStruct((n_slots, D), F32),
        compiler_params=_cparams(("arbitrary",)),
        name="moe_experts",
    )(ea, eb, ok, xs, wg, wg, wu, wu, wd, wd)


def _comb_kernel(slot_ref, x_ref, he_ref, g2_ref, wg_ref, wu_ref, wd_ref, y_ref, o_ref, ybuf, sem,
                 *, nt):
    base = (pl.program_id(0) * nt + pl.program_id(1)) * TM

    def issue(r, carry):
        _row_copy(y_ref, slot_ref[base + r], ybuf, r, sem).start()
        return carry

    lax.fori_loop(0, TM, issue, 0)
    h = he_ref[:, 0:D].astype(BF16)
    hid = _silu(jnp.dot(h, wg_ref[...], preferred_element_type=F32)) \
        * jnp.dot(h, wu_ref[...], preferred_element_type=F32)
    shared = jnp.dot(hid.astype(BF16), wd_ref[...], preferred_element_type=F32)

    def drain(r, carry):
        _row_copy(y_ref, 0, ybuf, 0, sem).wait()
        return carry

    lax.fori_loop(0, TM, drain, 0)
    o_ref[...] = x_ref[...] + g2_ref[...] * (shared + ybuf[...])


def _combine(slot, x, he, g2, wg, wu, wd, y, nct):
    bsz, s, _ = x.shape
    nt = s // TM

    def full(a):
        return pl.BlockSpec(a.shape, lambda b, t, sl: (0,) * a.ndim)

    tab = pl.BlockSpec((None, None, 1, D), lambda b, t, sl: (b, jnp.minimum(t // nct, 1), 0, 0))
    return pl.pallas_call(
        functools.partial(_comb_kernel, nt=nt),
        grid_spec=pltpu.PrefetchScalarGridSpec(
            num_scalar_prefetch=1,
            grid=(bsz, nt),
            in_specs=[pl.BlockSpec((None, TM, D), lambda b, t, sl: (b, t, 0)),
                      pl.BlockSpec((None, TM, D + EXT), lambda b, t, sl: (b, t, 0)),
                      tab, full(wg), full(wu), full(wd),
                      pl.BlockSpec(memory_space=pl.ANY)],
            out_specs=pl.BlockSpec((None, TM, D), lambda b, t, sl: (b, t, 0)),
            scratch_shapes=[pltpu.VMEM((TM, D), F32), pltpu.SemaphoreType.DMA(())]),
        out_shape=jax.ShapeDtypeStruct((bsz, s, D), F32),
        compiler_params=_cparams(("arbitrary", "arbitrary")),
        name="moe_combine",
    )(slot, x, he, g2, wg, wu, wd, y)


def _moe_layer(x, he, route, cnt, g2, wg, wu, wd, swg, swu, swd, nct):
    bsz, s, _ = x.shape
    n = bsz * s
    n_slots = (n + N_CLASS * (MB - 1) + MB - 1) // MB * MB
    cls = route[:, 0, :].reshape(n).astype(jnp.int32)
    rank = route[:, 1, :].reshape(n).astype(jnp.int32)
    counts = cnt[0:N_CLASS, 0].astype(jnp.int32)
    padded = (counts + MB - 1) // MB * MB
    pend = jnp.cumsum(padded)
    slot = (pend - padded)[cls] + rank
    bstart = jnp.arange(n_slots // MB, dtype=jnp.int32) * MB
    bcls = jnp.minimum(jnp.searchsorted(pend, bstart, side="right"), N_CLASS - 1).astype(jnp.int32)
    ok = (bstart < pend[-1]).astype(jnp.int32)
    pa = jnp.asarray([p[0] for p in PAIRS], jnp.int32)
    pb = jnp.asarray([p[1] for p in PAIRS], jnp.int32)
    ea = EPG * (bcls // len(PAIRS)) + pa[bcls % len(PAIRS)]
    eb = EPG * (bcls // len(PAIRS)) + pb[bcls % len(PAIRS)]
    xs = _dispatch(slot, he.reshape(n, D + EXT), n_slots)
    y = _moe_blocks(ea, eb, ok, xs, wg, wu, wd)
    return _combine(slot, x, he, g2, swg, swu, swd, y, nct)


def _rwkv_consts():
    t = np.arange(CH)
    low = (t[:, None] >= t[None, :]).astype(np.float32)
    tri = np.stack([low, low.T])
    blk = np.kron(np.eye(2, dtype=np.float32), np.ones((CH, CH), np.float32))
    t2 = np.arange(2 * CH) % CH
    gt = (t2[:, None] > t2[None, :]).astype(np.float32)
    ge = (t2[:, None] >= t2[None, :]).astype(np.float32)
    mstrict = np.stack([gt * blk, gt.T * blk])
    mincl = np.stack([ge * blk, ge.T * blk])
    bones = np.kron(np.eye(A_HEADS, dtype=np.float32), np.ones((A_HEAD, A_HEAD), np.float32))
    return {"tri": jnp.asarray(tri), "mstrict": jnp.asarray(mstrict), "mincl": jnp.asarray(mincl),
            "bones": jnp.asarray(bones)}


def _rope_tables(ctx_len, seq):
    rows = seq // GRID_W
    row = np.repeat(np.arange(rows), GRID_W).astype(np.float32)
    col = np.tile(np.arange(GRID_W), rows).astype(np.float32)
    half = B_ROPE // 2
    inv = (ROPE_BASE ** (-np.arange(0, half, 2, dtype=np.float32) / half)).astype(np.float32)
    ang = np.concatenate([row[:, None] * inv, col[:, None] * inv], axis=-1).astype(np.float32)
    cos = np.concatenate([np.ones((ctx_len, half), np.float32), np.cos(ang)], axis=0)
    sin = np.concatenate([np.zeros((ctx_len, half), np.float32), np.sin(ang)], axis=0)
    s = ctx_len + seq
    cc = np.zeros((s, HP), np.float32)
    s1 = np.zeros((s, HP), np.float32)
    s2 = np.zeros((s, HP), np.float32)
    cc[:, 0:B_NOPE] = 1.0
    cc[:, B_NOPE:B_NOPE + half] = cos
    cc[:, B_NOPE + half:B_QK] = cos
    s1[:, B_NOPE + half:B_QK] = sin
    s2[:, B_NOPE:B_NOPE + half] = -sin
    return jnp.asarray(cc), jnp.asarray(s1), jnp.asarray(s2)


def _dft_rows(n):
    idx = np.arange(n, dtype=np.int64)
    ang = 2.0 * np.pi * ((idx[:, None] * idx[None, :]) % n).astype(np.float64) / n
    scale = 1.0 / np.sqrt(n)
    return np.cos(ang) * scale, np.sin(ang) * scale


def _fourier_consts(ctx_len, seq):
    cc, sc = _dft_rows(C_GDIM)
    cs = np.concatenate([cc, sc], axis=1)
    cl, sl = _dft_rows(seq)
    cx, sx = _dft_rows(ctx_len)
    return (jnp.asarray(cs, BF16), jnp.asarray(np.concatenate([cl, -sl], axis=1), BF16),
            jnp.asarray(np.concatenate([cx, -sx], axis=1), BF16))


def _pad_heads(w, width, pad_to):
    kdim = w.shape[0]
    w = w.reshape(kdim, -1, width)
    w = jnp.pad(w, ((0, 0), (0, 0), (0, pad_to - width)))
    return w.reshape(kdim, -1)


def _even_params(j, w_in, a_mu, a_w0, a_w2, a_a0, a_a2, a_g2, a_k_k, a_k_a, a_r_k, a_ln_w, a_ln_b,
                 b_q_norm, b_w_q_up, b_kv_norm, b_w_kv_up, b_q_gain, b_k_gain, w_out_even):
    w = jnp.pad(w_in[j], ((0, 0), (0, PB_COLS - B_COLS))).astype(BF16)

    def lora_pad(m):
        z = jnp.zeros_like(m[0])
        return jnp.stack([jnp.concatenate([m[0], z], axis=0), jnp.concatenate([z, m[1]], axis=0)])

    rw = {"mu": a_mu[j][None, :],
          "w0": a_w0[j][:, None, :], "w2": lora_pad(a_w2[j]),
          "a0": a_a0[j][:, None, :], "a2": lora_pad(a_a2[j]),
          "g2": a_g2[j], "kk": a_k_k[j][None, :], "ka": a_k_a[j][None, :],
          "rk": a_r_k[j].reshape(1, A_W), "lnw": a_ln_w[j][None, :], "lnb": a_ln_b[j][None, :]}
    kvu = b_w_kv_up[j].reshape(B_KV_RANK, B_HEADS, B_NOPE + B_V)
    wk_nope = _pad_heads(kvu[:, :, :B_NOPE].reshape(B_KV_RANK, -1), B_NOPE, HP)
    wv = kvu[:, :, B_NOPE:].reshape(B_KV_RANK, -1)
    sel = np.zeros((B_ROPE, B_HEADS, HP), np.float32)
    for h in range(B_HEADS):
        sel[np.arange(B_ROPE), h, B_NOPE + np.arange(B_ROPE)] = 1.0
    krows = PB_COLS - B_Q_RANK
    wk = jnp.zeros((krows, B_HEADS * HP + B_HEADS * B_V), F32)
    wk = wk.at[0:B_KV_RANK, 0:B_HEADS * HP].set(wk_nope)
    wk = wk.at[B_KV_RANK:B_KV_RANK + B_ROPE, 0:B_HEADS * HP].set(jnp.asarray(sel.reshape(B_ROPE, -1)))
    wk = wk.at[0:B_KV_RANK, B_HEADS * HP:].set(wv)
    mla = {"qn": b_q_norm[j][None, :],
           "kvn": jnp.pad(b_kv_norm[j], (0, krows - B_KV_RANK), constant_values=1.0)[None, :],
           "wq": _pad_heads(b_w_q_up[j], B_QK, HP).astype(BF16),
           "wk": wk.astype(BF16),
           "qg": jnp.pad(b_q_gain[j], (0, HP - B_QK))[None, :],
           "kg": jnp.pad(b_k_gain[j], (0, HP - B_QK))[None, :]}
    return w, rw, mla, w_out_even[j].astype(BF16)


def kernel(x, c, ctx, c_ctx, norm1_g, norm2_g, mod_w, mod_b, w_in, a_mu, a_w0, a_w2, a_a0, a_a2, a_g2,
           a_k_k, a_k_a, a_r_k, a_ln_w, a_ln_b, b_q_norm, b_w_q_up, b_kv_norm, b_w_kv_up, b_q_gain,
           b_k_gain, w_out_even, w_out_fourier, router_w, router_b, moe_w_gate, moe_w_up, moe_w_down,
           shared_w_gate, shared_w_up, shared_w_down):
    bsz, seq, d = x.shape
    ctx_len = ctx.shape[1]
    assert d == D and seq % TM == 0 and ctx_len % TM == 0 and seq % GRID_W == 0
    nct = ctx_len // TM
    xs = jnp.concatenate([ctx, x], axis=1)

    rows = (bsz + 1 + 7) // 8 * 8
    cs = jnp.zeros((rows, D), F32).at[0:bsz].set(c).at[bsz].set(c_ctx)
    mod = _modulation(cs, mod_w, mod_b)

    def table(part):
        lat = part[0:bsz]
        cx = jnp.broadcast_to(part[bsz][None, :], (bsz, D))
        return jnp.stack([cx, lat], axis=1)[:, :, None, :]

    consts = _rwkv_consts()
    rope = _rope_tables(ctx_len, seq)
    cs_mat, dlat, dctx = _fourier_consts(ctx_len, seq)
    rwt = router_w.T
    rb = router_b[:, None]
    tt = np.arange(TM)
    striu = jnp.asarray((tt[:, None] < tt[None, :]).astype(np.float32), BF16)
    nctx_chunks = ctx_len // CH

    for i in range(DEPTH):
        j = i // 2
        sh1, sc1, g1, sh2, sc2, g2 = [mod[i][:, D * q:D * (q + 1)] for q in range(6)]
        t_sc1 = table((1.0 + sc1) * norm1_g[i][None, :])
        t_sh1 = table(sh1)
        t_g1 = table(g1)
        t_sc2 = table((1.0 + sc2) * norm2_g[i][None, :])
        t_sh2 = table(sh2)
        t_g2 = table(g2)
        if i % 2 == 0:
            w, rw, mla, wo = _even_params(
                j, w_in, a_mu, a_w0, a_w2, a_a0, a_a2, a_g2, a_k_k, a_k_a, a_r_k, a_ln_w, a_ln_b,
                b_q_norm, b_w_q_up, b_kv_norm, b_w_kv_up, b_q_gain, b_k_gain, w_out_even)
            pa, pb = _in_proj(xs, t_sc1, t_sh1, w, nct)
            yf = _rwkv_pass(pa, rw, consts, None, reverse=False, nctx=nctx_chunks)
            ya = _rwkv_pass(pa, rw, consts, yf, reverse=True, nctx=nctx_chunks)
            q, k, v = _mla_prep(pb, mla, rope)
            yb = _attention(q, k, v, ctx_len)
            xs, he, route, cnt = _even_out(ya, yb, wo, xs, t_g1, t_sc2, t_sh2, rwt, rb, striu, nct)
        else:
            hc, hs = _fourier_channels(xs, t_sc1, t_sh1, cs_mat, nct)
            xs, he, route, cnt = _fourier_out(hc, hs, dlat, dctx, w_out_fourier[j].astype(BF16), xs,
                                              t_g1, t_sc2, t_sh2, rwt, rb, striu, ctx_len)
        xs = _moe_layer(xs, he, route, cnt, t_g2,
                        moe_w_gate[i].astype(BF16), moe_w_up[i].astype(BF16), moe_w_down[i].astype(BF16),
                        shared_w_gate[i].astype(BF16), shared_w_up[i].astype(BF16),
                        shared_w_down[i].astype(BF16), nct)
    return xs[:, ctx_len:, :]
```

```python
import functools
import math

import numpy as np
import jax
import jax.numpy as jnp
from jax import lax
from jax.experimental import pallas as pl
from jax.experimental.pallas import tpu as pltpu

F32 = jnp.float32
BF16 = jnp.bfloat16
HI = lax.Precision.HIGHEST

D = 1024
DEPTH = 4
GRID_W = 64
NORM_EPS = 1e-6
A_HEAD = 64
A_HEADS = 8
A_W = 512
A_RANK = 64
A_GATE = 128
A_DECAY_SCALE = 0.606531
A_LN_EPS = 64e-5
A_COLS = 3 * A_W + 4 * A_RANK + A_GATE
B_HEADS = 8
B_NOPE = 64
B_ROPE = 32
B_QK = 96
B_V = 64
B_Q_RANK = 256
B_KV_RANK = 128
B_COLS = B_Q_RANK + B_KV_RANK + B_ROPE
ATTN_SCALE = B_QK ** -0.5
LOG2E = math.log2(math.e)
ROPE_BASE = 10000.0
C_GROUPS = 8
C_GDIM = 128
N_EXPERTS = 16
N_GROUPS = 4
EPG = 4
D_EXPERT = 512
D_SHARED = 512

LANE = 128
TM = 256
CH = 64
RW_NB = 4
MB = 256
PERM_ROWS = 2048
PB_COLS = 512
HP = 128
N_CLASS = 24
EXT = 128
PAIRS = ((0, 1), (0, 2), (0, 3), (1, 2), (1, 3), (2, 3))
VMEM_LIMIT = 56 * 1024 * 1024


def _cparams(sem):
    return pltpu.CompilerParams(dimension_semantics=sem, vmem_limit_bytes=VMEM_LIMIT)


def _mm(a, b):
    return jnp.dot(a.astype(BF16), b.astype(BF16), preferred_element_type=F32)


def _mm_nt(a, b):
    return lax.dot_general(a.astype(BF16), b.astype(BF16), (((1,), (1,)), ((), ())),
                           preferred_element_type=F32)


def _mm_tn(a, b):
    return lax.dot_general(a.astype(BF16), b.astype(BF16), (((0,), (0,)), ((), ())),
                           preferred_element_type=F32)


def _mm_hi(a, b):
    return jnp.dot(a, b, precision=HI, preferred_element_type=F32)


def _split_bf16(x):
    hi = x.astype(BF16)
    return hi, (x - hi.astype(F32)).astype(BF16)


def _mm_split_rhs(a, b):
    hi, lo = _split_bf16(b)
    a = a.astype(BF16)
    return jnp.dot(a, hi, preferred_element_type=F32) + jnp.dot(a, lo, preferred_element_type=F32)


def _mm_split_lhs(a, b):
    hi, lo = _split_bf16(a)
    b = b.astype(BF16)
    return jnp.dot(hi, b, preferred_element_type=F32) + jnp.dot(lo, b, preferred_element_type=F32)


def _sigmoid(x):
    return 1.0 / (1.0 + jnp.exp(-x))


def _silu(x):
    return x * _sigmoid(x)


def _rms_mod(x, scale, shift):
    ms = jnp.mean(x * x, axis=-1, keepdims=True)
    return x * lax.rsqrt(ms + NORM_EPS) * scale + shift


def _mod_kernel(c_ref, w_ref, b_ref, o_ref):
    o_ref[...] = _mm_hi(_silu(c_ref[...]), w_ref[...]) + b_ref[...]


def _modulation(cs, mod_w, mod_b):
    rows = cs.shape[0]
    tn = 1536
    return pl.pallas_call(
        _mod_kernel,
        grid=(DEPTH, 6 * D // tn),
        in_specs=[pl.BlockSpec((rows, D), lambda l, n: (0, 0)),
                  pl.BlockSpec((None, D, tn), lambda l, n: (l, 0, n)),
                  pl.BlockSpec((None, 1, tn), lambda l, n: (l, 0, n))],
        out_specs=pl.BlockSpec((None, rows, tn), lambda l, n: (l, 0, n)),
        out_shape=jax.ShapeDtypeStruct((DEPTH, rows, 6 * D), F32),
        compiler_params=_cparams(("arbitrary", "arbitrary")),
        name="modulation",
    )(cs, mod_w, mod_b.reshape(DEPTH, 1, 6 * D))


def _tab_spec(nct, t0=0):
    return pl.BlockSpec((None, None, 1, D),
                        lambda b, t, *_: (b, jnp.minimum((t + t0) // nct, 1), 0, 0))


def _e1_kernel(x_ref, sc_ref, sh_ref, w_ref, pa_ref, pb_ref):
    h = _rms_mod(x_ref[...], sc_ref[...], sh_ref[...])
    p = _mm(h, w_ref[...])
    pa_ref[...] = p[:, :A_COLS]
    pb_ref[...] = p[:, A_COLS:]


def _in_proj(x, sc, sh, w, nct):
    bsz, s, _ = x.shape
    ncol = A_COLS + PB_COLS
    return pl.pallas_call(
        _e1_kernel,
        grid=(bsz, s // TM),
        in_specs=[pl.BlockSpec((None, TM, D), lambda b, t: (b, t, 0)),
                  _tab_spec(nct), _tab_spec(nct),
                  pl.BlockSpec((D, ncol), lambda b, t: (0, 0))],
        out_specs=[pl.BlockSpec((None, TM, A_COLS), lambda b, t: (b, t, 0)),
                   pl.BlockSpec((None, TM, PB_COLS), lambda b, t: (b, t, 0))],
        out_shape=[jax.ShapeDtypeStruct((bsz, s, A_COLS), F32),
                   jax.ShapeDtypeStruct((bsz, s, PB_COLS), F32)],
        compiler_params=_cparams(("parallel", "parallel")),
        name="in_proj",
    )(x, sc, sh, w)


def _chunk_of(i, nctx, nch, reverse):
    if not reverse:
        return i
    return jnp.where(i < nctx, nctx - 1 - i, nch - 1 - (i - nctx))


def _rw_kernel(*refs, reverse, final, nctx, nch, nb):
    if final:
        (pc_ref, pp_ref, pn_ref, mu_ref, w0_ref, w2_ref, a0_ref, a2_ref, kk_ref, ka_ref,
         tri_ref, ms_ref, mi_ref, bones_ref,
         a0o_ref, a2o_ref, g2_ref, rk_ref, lnw_ref, lnb_ref, yf_ref, out_ref, z_ref) = refs
    else:
        (pc_ref, pp_ref, pn_ref, mu_ref, w0_ref, w2_ref, a0_ref, a2_ref, kk_ref, ka_ref,
         tri_ref, ms_ref, mi_ref, bones_ref, out_ref, z_ref) = refs
    i = pl.program_id(1)
    c = _chunk_of(i, nctx, nch, reverse)

    @pl.when(i == 0)
    def _():
        z_ref[...] = jnp.zeros_like(z_ref)

    seg_first = jnp.logical_or(c == 0, c == nctx)
    seg_last = jnp.logical_or(c == nctx - 1, c == nch - 1)
    rid = lax.broadcasted_iota(jnp.int32, (CH, 1), 0)
    bones = bones_ref[...]
    lane = lax.broadcasted_iota(jnp.int32, (1, LANE), 1)
    m0 = (lane < A_HEAD).astype(F32)
    m1 = 1.0 - m0
    eye = (lax.broadcasted_iota(jnp.int32, (LANE, LANE), 0)
           == lax.broadcasted_iota(jnp.int32, (LANE, LANE), 1))
    mstrict = ms_ref[...] > 0.5
    mincl = mi_ref[...] > 0.5

    def stack(x):
        return jnp.concatenate([x * m0, x * m1], axis=0)

    pre = []
    chains = []
    for n in range(nb):
        p = pc_ref[n]
        prow = jnp.where(seg_first, 0.0, pp_ref[n, 7:8, :])
        nrow = jnp.where(seg_last, 0.0, pn_ref[n, 0:1, :])
        pprev = jnp.where(rid == 0, prow, pltpu.roll(p, 1, 0))
        pnext = jnp.where(rid == CH - 1, nrow, pltpu.roll(p, CH - 1, 0))
        ps = p + mu_ref[...] * (0.5 * (pprev + pnext) - p)
        r = ps[:, 0:A_W]
        k = ps[:, A_W:2 * A_W]
        v = ps[:, 2 * A_W:3 * A_W]
        dd = ps[:, 3 * A_W:3 * A_W + 128]
        ad = ps[:, 3 * A_W + 128:3 * A_W + 256]
        gd = ps[:, 3 * A_W + 256:3 * A_W + 384]
        logw = -A_DECAY_SCALE * _sigmoid(w0_ref[...] + _mm(jnp.tanh(dd), w2_ref[...]))
        aic = _sigmoid(a0_ref[...] + _mm(ad, a2_ref[...]))
        kkr = k * kk_ref[...]
        kk = kkr * lax.rsqrt(_mm_split_lhs(kkr * kkr, bones) + 1e-12)
        kd = k * (1.0 + (aic - 1.0) * ka_ref[...])
        bd = kk * aic
        cum = _mm_split_rhs(tri_ref[...], logw)
        tot = cum[0:1, :] if reverse else cum[CH - 1:CH, :]
        rt = r * jnp.exp(cum)
        at = -kk * jnp.exp(cum - logw)
        ei = jnp.exp(-cum)
        kt = kd * ei
        bt = bd * ei
        et = jnp.exp(tot - cum)
        kh = kd * et
        bh = bd * et
        gam = jnp.exp(tot)
        pre.append((r, k, v, ad, gd, kd))
        for j in range(A_HEADS // 2):
            sl = slice(LANE * j, LANE * (j + 1))
            chains.append({
                "n": n, "j": j, "gam": gam[:, sl],
                "ats": stack(at[:, sl]), "rts": stack(rt[:, sl]), "bts": stack(bt[:, sl]),
                "kts": stack(kt[:, sl]), "vs": stack(v[:, sl]), "bhs": stack(bh[:, sl]),
                "khs": stack(kh[:, sl])})

    for ch in chains:
        big = _mm_nt(jnp.concatenate([ch["ats"], ch["rts"]], axis=0),
                     jnp.concatenate([ch["bts"], ch["kts"]], axis=0))
        ch["nm"] = jnp.where(mstrict, big[0:128, 0:128], 0.0)
        aak = jnp.where(mstrict, big[0:128, 128:256], 0.0)
        ch["arb"] = jnp.where(mincl, big[128:256, 0:128], 0.0)
        ark = jnp.where(mincl, big[128:256, 128:256], 0.0)
        ch["aa"] = jnp.concatenate([aak, ark], axis=0)
    for ch in chains:
        avs = _mm(ch["aa"], ch["vs"])
        ch["rhs"] = jnp.concatenate([ch["ats"], avs[0:128]], axis=1)
        ch["av2"] = avs[128:256]
        ch["t"] = jnp.where(eye, 1.0, ch["nm"])
    for it in range(5):
        for ch in chains:
            nmb = ch["nm"].astype(BF16)
            n2 = jnp.dot(nmb, nmb, preferred_element_type=F32)
            ch["t"] = ch["t"] + _mm(n2, ch["t"])
            ch["nm"] = n2
    for ch in chains:
        ch["x"] = _mm(ch["t"], ch["rhs"])
    ys = [[None] * (A_HEADS // 2) for _ in range(nb)]
    for ch in chains:
        x = ch["x"]
        qy = jnp.concatenate([ch["rts"], ch["av2"]], axis=1) + _mm(ch["arb"], x)
        qp = qy[0:CH, 0:128] + qy[CH:2 * CH, 0:128]
        y0 = qy[0:CH, 128:256] + qy[CH:2 * CH, 128:256]
        z = z_ref[ch["n"], ch["j"]]
        ys[ch["n"]][ch["j"]] = y0 + _mm(qp, z)
        lhs = jnp.concatenate([ch["bhs"], ch["khs"]], axis=0)
        rhs = jnp.concatenate(
            [x, jnp.concatenate([jnp.zeros((128, 128), F32), ch["vs"]], axis=1)], axis=0)
        m = _mm_tn(lhs, rhs)
        gt = jnp.where(eye, ch["gam"], 0.0) + m[:, 0:128]
        z_ref[ch["n"], ch["j"]] = _mm(gt, z) + m[:, 128:256]

    for n in range(nb):
        y = jnp.concatenate(ys[n], axis=1)
        if not final:
            out_ref[n] = y
        else:
            r, k, v, ad, gd, kd = pre[n]
            y = y + yf_ref[n]
            mean = _mm_split_lhs(y, bones) * (1.0 / A_HEAD)
            yc = y - mean
            var = _mm_split_lhs(yc * yc, bones) * (1.0 / A_HEAD)
            yn = yc * lax.rsqrt(var + A_LN_EPS) * lnw_ref[...] + lnb_ref[...]
            aico = _sigmoid(a0o_ref[...] + _mm(ad, a2o_ref[...]))
            kdo = k * (1.0 + (aico - 1.0) * ka_ref[...])
            bonus = _mm_split_lhs(r * rk_ref[...] * (kd + kdo), bones) * v
            g = _mm(_sigmoid(gd), g2_ref[...])
            out_ref[n] = ((yn + bonus) * g).astype(out_ref.dtype)


def _rwkv_pass(pa, prm, consts, yf, *, reverse, nctx):
    bsz, s, _ = pa.shape
    nch = s // CH
    final = yf is not None
    d = 1 if reverse else 0
    nb = RW_NB
    assert bsz % nb == 0

    def cmap(b, i):
        return (b, _chunk_of(i, nctx, nch, reverse), 0)

    def pmap(b, i):
        return (b, jnp.maximum(_chunk_of(i, nctx, nch, reverse) * (CH // 8) - 1, 0), 0)

    def nmap(b, i):
        return (b, jnp.minimum((_chunk_of(i, nctx, nch, reverse) + 1) * (CH // 8), s // 8 - 1), 0)

    def full(a):
        return pl.BlockSpec(a.shape, lambda b, i: (0,) * a.ndim)

    args = [pa, pa, pa, prm["mu"], prm["w0"][d], prm["w2"][d], prm["a0"][d], prm["a2"][d],
            prm["kk"], prm["ka"], consts["tri"][d], consts["mstrict"][d], consts["mincl"][d],
            consts["bones"]]
    specs = [pl.BlockSpec((nb, CH, A_COLS), cmap),
             pl.BlockSpec((nb, 8, A_COLS), pmap),
             pl.BlockSpec((nb, 8, A_COLS), nmap)] + [full(a) for a in args[3:]]
    if final:
        extra = [prm["a0"][1 - d], prm["a2"][1 - d], prm["g2"], prm["rk"], prm["lnw"], prm["lnb"]]
        args += extra + [yf]
        specs += [full(a) for a in extra] + [pl.BlockSpec((nb, CH, A_W), cmap)]
    return pl.pallas_call(
        functools.partial(_rw_kernel, reverse=reverse, final=final, nctx=nctx, nch=nch, nb=nb),
        grid=(bsz // nb, nch),
        in_specs=specs,
        out_specs=pl.BlockSpec((nb, CH, A_W), cmap),
        out_shape=jax.ShapeDtypeStruct((bsz, s, A_W), BF16 if final else F32),
        scratch_shapes=[pltpu.VMEM((nb, A_HEADS // 2, LANE, LANE), F32)],
        compiler_params=_cparams(("parallel", "arbitrary")),
        name="rwkv_bwd" if reverse else "rwkv_fwd",
    )(*args)


def _mla_prep_kernel(pb_ref, qn_ref, kvn_ref, wq_ref, wk_ref, qg_ref, kg_ref,
                     c_ref, s1_ref, s2_ref, q_ref, k_ref, v_ref):
    pb = pb_ref[...]
    pq = pb[:, 0:B_Q_RANK]
    xq = pq * lax.rsqrt(jnp.mean(pq * pq, axis=-1, keepdims=True) + NORM_EPS) * qn_ref[...]
    q = _mm(xq, wq_ref[...])
    pk = pb[:, B_Q_RANK:PB_COLS]
    is_kv = lax.broadcasted_iota(jnp.int32, (1, PB_COLS - B_Q_RANK), 1) < B_KV_RANK
    ms = jnp.sum(jnp.where(is_kv, pk * pk, 0.0), axis=-1, keepdims=True) * (1.0 / B_KV_RANK)
    xk = jnp.where(is_kv, pk * lax.rsqrt(ms + NORM_EPS) * kvn_ref[...], pk)
    kv = _mm(xk, wk_ref[...])
    cc, s1, s2 = c_ref[...], s1_ref[...], s2_ref[...]

    def head(xh, gain):
        ss = jnp.sum(xh * xh, axis=-1, keepdims=True) * (1.0 / B_QK)
        xh = xh * lax.rsqrt(ss + NORM_EPS) * gain
        return xh * cc + pltpu.roll(xh, B_ROPE // 2, 1) * s1 + pltpu.roll(xh, HP - B_ROPE // 2, 1) * s2

    for h in range(B_HEADS):
        sl = slice(HP * h, HP * (h + 1))
        q_ref[:, sl] = (head(q[:, sl], qg_ref[...]) * (ATTN_SCALE * LOG2E)).astype(BF16)
        k_ref[:, sl] = head(kv[:, sl], kg_ref[...]).astype(BF16)
    v_ref[...] = kv[:, B_HEADS * HP:].astype(BF16)


def _mla_prep(pb, prm, rope):
    bsz, s, _ = pb.shape

    def full(a):
        return pl.BlockSpec(a.shape, lambda b, t: (0,) * a.ndim)

    consts = [prm["qn"], prm["kvn"], prm["wq"], prm["wk"], prm["qg"], prm["kg"]]
    rspec = pl.BlockSpec((TM, HP), lambda b, t: (t, 0))
    return pl.pallas_call(
        _mla_prep_kernel,
        grid=(bsz, s // TM),
        in_specs=[pl.BlockSpec((None, TM, PB_COLS), lambda b, t: (b, t, 0))]
        + [full(a) for a in consts] + [rspec, rspec, rspec],
        out_specs=[pl.BlockSpec((None, TM, B_HEADS * HP), lambda b, t: (b, t, 0)),
                   pl.BlockSpec((None, TM, B_HEADS * HP), lambda b, t: (b, t, 0)),
                   pl.BlockSpec((None, TM, B_HEADS * B_V), lambda b, t: (b, t, 0))],
        out_shape=[jax.ShapeDtypeStruct((bsz, s, B_HEADS * HP), BF16),
                   jax.ShapeDtypeStruct((bsz, s, B_HEADS * HP), BF16),
                   jax.ShapeDtypeStruct((bsz, s, B_HEADS * B_V), BF16)],
        compiler_params=_cparams(("parallel", "parallel")),
        name="mla_prep",
    )(pb, *consts, *rope)


def _att_kernel(q_ref, k_ref, v_ref, o_ref, *, ctx_len, nct):
    t = pl.program_id(2)
    lane = lax.broadcasted_iota(jnp.int32, (1, 2 * B_V), 1)

    def run(nk):
        outs = []
        for s in range(2):
            q = q_ref[:, HP * s:HP * (s + 1)]
            k = k_ref[0:nk, HP * s:HP * (s + 1)]
            sc = lax.dot_general(q, k, (((1,), (1,)), ((), ())), preferred_element_type=F32)
            p = jnp.exp2(sc - jnp.max(sc, axis=-1, keepdims=True)).astype(BF16)
            ones_lane = B_V * (1 - s)
            vm = jnp.where((lane >= B_V * s) & (lane < B_V * (s + 1)), v_ref[0:nk, :],
                           jnp.where(lane == ones_lane, 1.0, 0.0).astype(BF16))
            acc = jnp.dot(p, vm, preferred_element_type=F32)
            outs.append(acc / acc[:, ones_lane:ones_lane + 1])
        o_ref[...] = jnp.where(lane < B_V, outs[0], outs[1]).astype(BF16)

    @pl.when(t < nct)
    def _():
        run(ctx_len)

    @pl.when(t >= nct)
    def _():
        run(k_ref.shape[0])


def _attention(q, k, v, ctx_len):
    bsz, s, _ = q.shape
    nct = ctx_len // TM
    return pl.pallas_call(
        functools.partial(_att_kernel, ctx_len=ctx_len, nct=nct),
        grid=(bsz, B_HEADS // 2, s // TM),
        in_specs=[pl.BlockSpec((None, TM, 2 * HP), lambda b, j, t: (b, t, j)),
                  pl.BlockSpec((None, s, 2 * HP), lambda b, j, t: (b, 0, j)),
                  pl.BlockSpec((None, s, 2 * B_V), lambda b, j, t: (b, 0, j))],
        out_specs=pl.BlockSpec((None, TM, 2 * B_V), lambda b, j, t: (b, t, j)),
        out_shape=jax.ShapeDtypeStruct((bsz, s, B_HEADS * B_V), BF16),
        compiler_params=_cparams(("parallel", "parallel", "parallel")),
        name="mla_attention",
    )(q, k, v)


def _route_tail(x, o, g1, sc2, sh2, rwh_ref, rwl_ref, rb_ref, pm_ref, pab_ref, stril_ref,
                xo_ref, he_ref, cnt_ref, cnt_scr, first):
    xn = x + g1 * o
    xo_ref[...] = xn
    h2 = _rms_mod(xn, sc2, sh2)
    he_ref[:, 0:D] = h2

    @pl.when(first)
    def _():
        cnt_scr[...] = jnp.zeros_like(cnt_scr)

    lane = lax.broadcasted_iota(jnp.int32, (1, LANE), 1)
    h_hi, h_lo = _split_bf16(h2)
    r_hi, r_lo = rwh_ref[...], rwl_ref[...]
    logits = (jnp.dot(h_hi, r_hi, preferred_element_type=F32)
              + jnp.dot(h_lo, r_hi, preferred_element_type=F32)
              + jnp.dot(h_hi, r_lo, preferred_element_type=F32))
    scores = _sigmoid(logits)
    sel = scores + rb_ref[...]
    ps = jnp.where(lane < N_CLASS, _mm_split_lhs(sel, pm_ref[...]), -jnp.inf)
    best = jnp.max(ps, axis=-1, keepdims=True)
    cls = jnp.min(jnp.where(ps == best, lane, LANE), axis=-1, keepdims=True)
    onehot = (lane == cls).astype(F32)
    s_hi, s_lo = _split_bf16(scores)
    sab = (jnp.dot(s_hi, pab_ref[...], preferred_element_type=F32)
           + jnp.dot(s_lo, pab_ref[...], preferred_element_type=F32))
    sa = jnp.sum(onehot * sab[:, 0:LANE], axis=-1, keepdims=True)
    sb = jnp.sum(onehot * sab[:, LANE:2 * LANE], axis=-1, keepdims=True)
    den = sa + sb
    prefix = jnp.dot(stril_ref[...], onehot.astype(BF16), preferred_element_type=F32)
    rank = jnp.sum(onehot * (prefix + cnt_scr[0:1, :]), axis=-1, keepdims=True)
    cnt_scr[...] = cnt_scr[...] + jnp.sum(onehot, axis=0, keepdims=True)
    cnt_ref[...] = cnt_scr[...]
    he_ref[:, D:D + EXT] = jnp.where(
        lane == 0, sa / den,
        jnp.where(lane == 1, sb / den,
                  jnp.where(lane == 2, cls.astype(F32), jnp.where(lane == 3, rank, 0.0))))


def _tail_out(bsz, s_out):
    specs = [pl.BlockSpec((None, TM, D), lambda b, t: (b, t, 0)),
             pl.BlockSpec((None, TM, D + EXT), lambda b, t: (b, t, 0)),
             pl.BlockSpec((8, LANE), lambda b, t: (0, 0))]
    shapes = [jax.ShapeDtypeStruct((bsz, s_out, D), F32),
              jax.ShapeDtypeStruct((bsz, s_out, D + EXT), F32),
              jax.ShapeDtypeStruct((8, LANE), F32)]
    return specs, shapes


def _e2_kernel(ya_ref, yb_ref, wo_ref, x_ref, g1_ref, sc_ref, sh_ref, rwh_ref, rwl_ref, rb_ref, pm_ref,
               pab_ref, stril_ref, xo_ref, he_ref, cnt_ref, cnt_scr):
    o = (jnp.dot(ya_ref[...], wo_ref[0:A_W, :], preferred_element_type=F32)
         + jnp.dot(yb_ref[...], wo_ref[A_W:2 * A_W, :], preferred_element_type=F32))
    first = jnp.logical_and(pl.program_id(0) == 0, pl.program_id(1) == 0)
    _route_tail(x_ref[...], o, g1_ref[...], sc_ref[...], sh_ref[...], rwh_ref, rwl_ref, rb_ref, pm_ref,
                pab_ref, stril_ref, xo_ref, he_ref, cnt_ref, cnt_scr, first)


def _even_out(ya, yb, wo, x, g1, sc2, sh2, rt, nct):
    bsz, s, _ = x.shape
    out_specs, out_shape = _tail_out(bsz, s)

    def full(a):
        return pl.BlockSpec(a.shape, lambda b, t: (0,) * a.ndim)

    return pl.pallas_call(
        _e2_kernel,
        grid=(bsz, s // TM),
        in_specs=[pl.BlockSpec((None, TM, A_W), lambda b, t: (b, t, 0)),
                  pl.BlockSpec((None, TM, A_W), lambda b, t: (b, t, 0)),
                  full(wo),
                  pl.BlockSpec((None, TM, D), lambda b, t: (b, t, 0)),
                  _tab_spec(nct), _tab_spec(nct), _tab_spec(nct)] + [full(a) for a in rt],
        out_specs=out_specs,
        out_shape=out_shape,
        scratch_shapes=[pltpu.VMEM((8, LANE), F32)],
        compiler_params=_cparams(("arbitrary", "arbitrary")),
        name="even_out_route",
    )(ya, yb, wo, x, g1, sc2, sh2, *rt)


def _f1_kernel(x_ref, sc_ref, sh_ref, cs_ref, hc_ref, hs_ref):
    h = _rms_mod(x_ref[...], sc_ref[...], sh_ref[...]).astype(BF16)
    for g in range(C_GROUPS):
        sl = slice(C_GDIM * g, C_GDIM * (g + 1))
        r = jnp.dot(h[:, sl], cs_ref[...], preferred_element_type=F32)
        hc_ref[:, sl] = r[:, 0:C_GDIM].astype(BF16)
        hs_ref[:, sl] = r[:, C_GDIM:2 * C_GDIM].astype(BF16)


def _fourier_channels(x, sc, sh, cs, nct):
    bsz, s, _ = x.shape
    return pl.pallas_call(
        _f1_kernel,
        grid=(bsz, s // TM),
        in_specs=[pl.BlockSpec((None, TM, D), lambda b, t: (b, t, 0)),
                  _tab_spec(nct), _tab_spec(nct),
                  pl.BlockSpec(cs.shape, lambda b, t: (0, 0))],
        out_specs=[pl.BlockSpec((None, TM, D), lambda b, t: (b, t, 0)),
                   pl.BlockSpec((None, TM, D), lambda b, t: (b, t, 0))],
        out_shape=[jax.ShapeDtypeStruct((bsz, s, D), BF16),
                   jax.ShapeDtypeStruct((bsz, s, D), BF16)],
        compiler_params=_cparams(("parallel", "parallel")),
        name="fourier_channels",
    )(x, sc, sh, cs)


def _f2_kernel(hc_ref, hs_ref, dl_ref, dc_ref, wo_ref, x_ref, g1_ref, sc_ref, sh_ref, rwh_ref, rwl_ref,
               rb_ref, pm_ref, pab_ref, stril_ref, xo_ref, he_ref, cnt_ref, cnt_scr, f_scr,
               *, ctx_len, nct, t0):
    t = pl.program_id(1) + t0
    s = hc_ref.shape[0]
    seq = s - ctx_len

    if t0 < nct:
        @pl.when(t < nct)
        def _():
            f_scr[...] = (jnp.dot(dc_ref[:, 0:ctx_len], hc_ref[0:ctx_len, :], preferred_element_type=F32)
                          + jnp.dot(dc_ref[:, ctx_len:2 * ctx_len], hs_ref[0:ctx_len, :],
                                    preferred_element_type=F32))

    @pl.when(t >= nct)
    def _():
        f_scr[...] = (jnp.dot(dl_ref[:, 0:seq], hc_ref[ctx_len:s, :], preferred_element_type=F32)
                      + jnp.dot(dl_ref[:, seq:2 * seq], hs_ref[ctx_len:s, :],
                                preferred_element_type=F32))

    o = _mm(f_scr[...], wo_ref[...])
    first = jnp.logical_and(pl.program_id(0) == 0, pl.program_id(1) == 0)
    _route_tail(x_ref[...], o, g1_ref[...], sc_ref[...], sh_ref[...], rwh_ref, rwl_ref, rb_ref, pm_ref,
                pab_ref, stril_ref, xo_ref, he_ref, cnt_ref, cnt_scr, first)


def _fourier_out(hc, hs, dlat, dctx, wo, x, g1, sc2, sh2, rt, ctx_len, t0):
    bsz, s, _ = x.shape
    nct = ctx_len // TM
    nt = s // TM - t0
    out_specs, out_shape = _tail_out(bsz, nt * TM)

    def full(a):
        return pl.BlockSpec(a.shape, lambda b, t: (0,) * a.ndim)

    return pl.pallas_call(
        functools.partial(_f2_kernel, ctx_len=ctx_len, nct=nct, t0=t0),
        grid=(bsz, nt),
        in_specs=[pl.BlockSpec((None, s, D), lambda b, t: (b, 0, 0)),
                  pl.BlockSpec((None, s, D), lambda b, t: (b, 0, 0)),
                  pl.BlockSpec((TM, dlat.shape[1]), lambda b, t: (jnp.maximum(t + t0 - nct, 0), 0)),
                  pl.BlockSpec((TM, dctx.shape[1]), lambda b, t: (jnp.minimum(t + t0, nct - 1), 0)),
                  full(wo),
                  pl.BlockSpec((None, TM, D), lambda b, t: (b, t + t0, 0)),
                  _tab_spec(nct, t0), _tab_spec(nct, t0), _tab_spec(nct, t0)]
        + [full(a) for a in rt],
        out_specs=out_specs,
        out_shape=out_shape,
        scratch_shapes=[pltpu.VMEM((8, LANE), F32), pltpu.VMEM((TM, D), F32)],
        compiler_params=_cparams(("arbitrary", "arbitrary")),
        name="fourier_out_route",
    )(hc, hs, dlat, dctx, wo, x, g1, sc2, sh2, *rt)


def _row_copy(src_ref, src_row, dst_ref, dst_row, sem):
    return pltpu.make_async_copy(src_ref.at[pl.ds(src_row, 1)], dst_ref.at[pl.ds(dst_row, 1)], sem)


def _perm_kernel(idx_ref, src_ref, *rest, scatter, rows):
    dst_ref, sem = rest[-2], rest[-1]
    base = pl.program_id(0) * rows

    def issue(r, carry):
        row = base + r
        if scatter:
            _row_copy(src_ref, row, dst_ref, idx_ref[row], sem).start()
        else:
            _row_copy(src_ref, idx_ref[row], dst_ref, row, sem).start()
        return carry

    lax.fori_loop(0, rows, issue, 0, unroll=8)
    pltpu.make_async_copy(src_ref.at[pl.ds(0, rows)], dst_ref.at[pl.ds(0, rows)], sem).wait()


def _scatter_rows(slot, src, dst):
    n = src.shape[0]
    rows = math.gcd(n, PERM_ROWS)
    return pl.pallas_call(
        functools.partial(_perm_kernel, scatter=True, rows=rows),
        grid_spec=pltpu.PrefetchScalarGridSpec(
            num_scalar_prefetch=1,
            grid=(n // rows,),
            in_specs=[pl.BlockSpec(memory_space=pl.ANY), pl.BlockSpec(memory_space=pl.ANY)],
            out_specs=pl.BlockSpec(memory_space=pl.ANY),
            scratch_shapes=[pltpu.SemaphoreType.DMA(())]),
        out_shape=jax.ShapeDtypeStruct(dst.shape, dst.dtype),
        input_output_aliases={2: 0},
        compiler_params=_cparams(("arbitrary",)),
        name="moe_scatter_rows",
    )(slot, src, dst)


def _gather_rows(slot, src, n):
    rows = math.gcd(n, PERM_ROWS)
    return pl.pallas_call(
        functools.partial(_perm_kernel, scatter=False, rows=rows),
        grid_spec=pltpu.PrefetchScalarGridSpec(
            num_scalar_prefetch=1,
            grid=(n // rows,),
            in_specs=[pl.BlockSpec(memory_space=pl.ANY)],
            out_specs=pl.BlockSpec(memory_space=pl.ANY),
            scratch_shapes=[pltpu.SemaphoreType.DMA(())]),
        out_shape=jax.ShapeDtypeStruct((n, src.shape[1]), src.dtype),
        compiler_params=_cparams(("arbitrary",)),
        name="moe_gather_rows",
    )(slot, src)


def _moe_kernel(ea_ref, eb_ref, ok_ref, xs_ref, wga_ref, wgb_ref, wua_ref, wub_ref, wda_ref, wdb_ref,
                y_ref):
    i = pl.program_id(0)

    @pl.when(ok_ref[i] > 0)
    def _():
        xe = xs_ref[...]
        x = xe[:, 0:D].astype(BF16)
        wa = xe[:, D:D + 1]
        wb = xe[:, D + 1:D + 2]
        ha = _silu(jnp.dot(x, wga_ref[...], preferred_element_type=F32)) \
            * jnp.dot(x, wua_ref[...], preferred_element_type=F32) * wa
        hb = _silu(jnp.dot(x, wgb_ref[...], preferred_element_type=F32)) \
            * jnp.dot(x, wub_ref[...], preferred_element_type=F32) * wb
        y_ref[...] = (jnp.dot(ha.astype(BF16), wda_ref[...], preferred_element_type=F32)
                      + jnp.dot(hb.astype(BF16), wdb_ref[...], preferred_element_type=F32))

    @pl.when(ok_ref[i] == 0)
    def _():
        y_ref[...] = jnp.zeros_like(y_ref)


def _moe_blocks(ea, eb, ok, xs, wg, wu, wd):
    n_slots = xs.shape[0]
    wa_spec = pl.BlockSpec((None, D, D_EXPERT), lambda i, ea, eb, ok: (ea[i], 0, 0))
    wb_spec = pl.BlockSpec((None, D, D_EXPERT), lambda i, ea, eb, ok: (eb[i], 0, 0))
    da_spec = pl.BlockSpec((None, D_EXPERT, D), lambda i, ea, eb, ok: (ea[i], 0, 0))
    db_spec = pl.BlockSpec((None, D_EXPERT, D), lambda i, ea, eb, ok: (eb[i], 0, 0))
    return pl.pallas_call(
        _moe_kernel,
        grid_spec=pltpu.PrefetchScalarGridSpec(
            num_scalar_prefetch=3,
            grid=(n_slots // MB,),
            in_specs=[pl.BlockSpec((MB, D + EXT), lambda i, ea, eb, ok: (i, 0)),
                      wa_spec, wb_spec, wa_spec, wb_spec, da_spec, db_spec],
            out_specs=pl.BlockSpec((MB, D), lambda i, ea, eb, ok: (i, 0))),
        out_shape=jax.ShapeDtypeStruct((n_slots, D), F32),
        compiler_params=_cparams(("arbitrary",)),
        name="moe_experts",
    )(ea, eb, ok, xs, wg, wg, wu, wu, wd, wd)


def _comb_kernel(x_ref, he_ref, y_ref, g2_ref, wg_ref, wu_ref, wd_ref, o_ref):
    h = he_ref[:, 0:D].astype(BF16)
    hid = _silu(jnp.dot(h, wg_ref[...], preferred_element_type=F32)) \
        * jnp.dot(h, wu_ref[...], preferred_element_type=F32)
    shared = jnp.dot(hid.astype(BF16), wd_ref[...], preferred_element_type=F32)
    o_ref[...] = x_ref[...] + g2_ref[...] * (shared + y_ref[...])


def _combine(x, he, y, g2, wg, wu, wd, nct, t0):
    bsz, s, _ = x.shape

    def full(a):
        return pl.BlockSpec(a.shape, lambda b, t: (0,) * a.ndim)

    return pl.pallas_call(
        _comb_kernel,
        grid=(bsz, s // TM),
        in_specs=[pl.BlockSpec((None, TM, D), lambda b, t: (b, t, 0)),
                  pl.BlockSpec((None, TM, D + EXT), lambda b, t: (b, t, 0)),
                  pl.BlockSpec((None, TM, D), lambda b, t: (b, t, 0)),
                  _tab_spec(nct, t0), full(wg), full(wu), full(wd)],
        out_specs=pl.BlockSpec((None, TM, D), lambda b, t: (b, t, 0)),
        out_shape=jax.ShapeDtypeStruct((bsz, s, D), F32),
        compiler_params=_cparams(("parallel", "parallel")),
        name="moe_combine",
    )(x, he, y, g2, wg, wu, wd)


def _moe_layer(x, he, cnt, xs_buf, g2, wg, wu, wd, swg, swu, swd, nct, t0):
    bsz, s, _ = x.shape
    n = bsz * s
    n_slots = xs_buf.shape[0]
    cls = he[:, :, D + 2].reshape(n).astype(jnp.int32)
    rank = he[:, :, D + 3].reshape(n).astype(jnp.int32)
    counts = cnt[0, 0:N_CLASS].astype(jnp.int32)
    padded = (counts + MB - 1) // MB * MB
    pend = jnp.cumsum(padded)
    slot = (pend - padded)[cls] + rank
    bstart = jnp.arange(n_slots // MB, dtype=jnp.int32) * MB
    bcls = jnp.minimum(jnp.sum((bstart[:, None] >= pend[None, :]).astype(jnp.int32), axis=1),
                       N_CLASS - 1)
    ok = (bstart < pend[-1]).astype(jnp.int32)
    pa = jnp.asarray([p[0] for p in PAIRS], jnp.int32)
    pb = jnp.asarray([p[1] for p in PAIRS], jnp.int32)
    ea = EPG * (bcls // len(PAIRS)) + pa[bcls % len(PAIRS)]
    eb = EPG * (bcls // len(PAIRS)) + pb[bcls % len(PAIRS)]
    xs_buf = _scatter_rows(slot, he.reshape(n, D + EXT), xs_buf)
    y = _moe_blocks(ea, eb, ok, xs_buf, wg, wu, wd)
    ytok = _gather_rows(slot, y, n).reshape(bsz, s, D)
    return _combine(x, he, ytok, g2, swg, swu, swd, nct, t0), xs_buf


def _rwkv_consts():
    t = np.arange(CH)
    low = (t[:, None] >= t[None, :]).astype(np.float32)
    tri = np.stack([low, low.T])
    blk = np.kron(np.eye(2, dtype=np.float32), np.ones((CH, CH), np.float32))
    t2 = np.arange(2 * CH) % CH
    gt = (t2[:, None] > t2[None, :]).astype(np.float32)
    ge = (t2[:, None] >= t2[None, :]).astype(np.float32)
    mstrict = np.stack([gt * blk, gt.T * blk])
    mincl = np.stack([ge * blk, ge.T * blk])
    bones = np.kron(np.eye(A_HEADS, dtype=np.float32), np.ones((A_HEAD, A_HEAD), np.float32))
    return {"tri": jnp.asarray(tri), "mstrict": jnp.asarray(mstrict), "mincl": jnp.asarray(mincl),
            "bones": jnp.asarray(bones)}


def _route_consts(router_w, router_b):
    pm = np.zeros((LANE, LANE), np.float32)
    pa = np.zeros((LANE, LANE), np.float32)
    pb = np.zeros((LANE, LANE), np.float32)
    for cidx in range(N_CLASS):
        g, (a, b) = cidx // len(PAIRS), PAIRS[cidx % len(PAIRS)]
        pm[EPG * g + a, cidx] = 1.0
        pm[EPG * g + b, cidx] = 1.0
        pa[EPG * g + a, cidx] = 1.0
        pb[EPG * g + b, cidx] = 1.0
    tt = np.arange(TM)
    stril = (tt[:, None] > tt[None, :]).astype(np.float32)
    rw = jnp.pad(router_w, ((0, 0), (0, LANE - N_EXPERTS)))
    rw_hi = rw.astype(BF16)
    rw_lo = (rw - rw_hi.astype(F32)).astype(BF16)
    rb = jnp.pad(router_b, (0, LANE - N_EXPERTS))[None, :]
    pab = np.concatenate([pa, pb], axis=1)
    return [rw_hi, rw_lo, rb, jnp.asarray(pm, BF16), jnp.asarray(pab, BF16), jnp.asarray(stril, BF16)]


def _rope_tables(ctx_len, seq):
    rows = seq // GRID_W
    row = np.repeat(np.arange(rows), GRID_W).astype(np.float32)
    col = np.tile(np.arange(GRID_W), rows).astype(np.float32)
    half = B_ROPE // 2
    inv = (ROPE_BASE ** (-np.arange(0, half, 2, dtype=np.float32) / half)).astype(np.float32)
    ang = np.concatenate([row[:, None] * inv, col[:, None] * inv], axis=-1).astype(np.float32)
    cos = np.concatenate([np.ones((ctx_len, half), np.float32), np.cos(ang)], axis=0)
    sin = np.concatenate([np.zeros((ctx_len, half), np.float32), np.sin(ang)], axis=0)
    s = ctx_len + seq
    cc = np.zeros((s, HP), np.float32)
    s1 = np.zeros((s, HP), np.float32)
    s2 = np.zeros((s, HP), np.float32)
    cc[:, 0:B_NOPE] = 1.0
    cc[:, B_NOPE:B_NOPE + half] = cos
    cc[:, B_NOPE + half:B_QK] = cos
    s1[:, B_NOPE + half:B_QK] = sin
    s2[:, B_NOPE:B_NOPE + half] = -sin
    return jnp.asarray(cc), jnp.asarray(s1), jnp.asarray(s2)


def _dft_rows(n):
    idx = np.arange(n, dtype=np.int64)
    ang = 2.0 * np.pi * ((idx[:, None] * idx[None, :]) % n).astype(np.float64) / n
    scale = 1.0 / np.sqrt(n)
    return np.cos(ang) * scale, np.sin(ang) * scale


def _fourier_consts(ctx_len, seq):
    cc, sc = _dft_rows(C_GDIM)
    cs = np.concatenate([cc, sc], axis=1)
    cl, sl = _dft_rows(seq)
    cx, sx = _dft_rows(ctx_len)
    return (jnp.asarray(cs, BF16), jnp.asarray(np.concatenate([cl, -sl], axis=1), BF16),
            jnp.asarray(np.concatenate([cx, -sx], axis=1), BF16))


def _pad_heads(w, width, pad_to):
    kdim = w.shape[0]
    w = w.reshape(kdim, -1, width)
    w = jnp.pad(w, ((0, 0), (0, 0), (0, pad_to - width)))
    return w.reshape(kdim, -1)


def _even_params(j, w_in, a_mu, a_w0, a_w2, a_a0, a_a2, a_g2, a_k_k, a_k_a, a_r_k, a_ln_w, a_ln_b,
                 b_q_norm, b_w_q_up, b_kv_norm, b_w_kv_up, b_q_gain, b_k_gain, w_out_even):
    w = jnp.pad(w_in[j], ((0, 0), (0, PB_COLS - B_COLS))).astype(BF16)

    def lora_pad(m):
        z = jnp.zeros_like(m[0])
        return jnp.stack([jnp.concatenate([m[0], z], axis=0), jnp.concatenate([z, m[1]], axis=0)])

    rw = {"mu": a_mu[j][None, :],
          "w0": a_w0[j][:, None, :], "w2": lora_pad(a_w2[j]),
          "a0": a_a0[j][:, None, :], "a2": lora_pad(a_a2[j]),
          "g2": a_g2[j], "kk": a_k_k[j][None, :], "ka": a_k_a[j][None, :],
          "rk": a_r_k[j].reshape(1, A_W), "lnw": a_ln_w[j][None, :], "lnb": a_ln_b[j][None, :]}
    kvu = b_w_kv_up[j].reshape(B_KV_RANK, B_HEADS, B_NOPE + B_V)
    wk_nope = _pad_heads(kvu[:, :, :B_NOPE].reshape(B_KV_RANK, -1), B_NOPE, HP)
    wv = kvu[:, :, B_NOPE:].reshape(B_KV_RANK, -1)
    sel = np.zeros((B_ROPE, B_HEADS, HP), np.float32)
    for h in range(B_HEADS):
        sel[np.arange(B_ROPE), h, B_NOPE + np.arange(B_ROPE)] = 1.0
    krows = PB_COLS - B_Q_RANK
    wk = jnp.zeros((krows, B_HEADS * HP + B_HEADS * B_V), F32)
    wk = wk.at[0:B_KV_RANK, 0:B_HEADS * HP].set(wk_nope)
    wk = wk.at[B_KV_RANK:B_KV_RANK + B_ROPE, 0:B_HEADS * HP].set(jnp.asarray(sel.reshape(B_ROPE, -1)))
    wk = wk.at[0:B_KV_RANK, B_HEADS * HP:].set(wv)
    mla = {"qn": b_q_norm[j][None, :],
           "kvn": jnp.pad(b_kv_norm[j], (0, krows - B_KV_RANK), constant_values=1.0)[None, :],
           "wq": _pad_heads(b_w_q_up[j], B_QK, HP).astype(BF16),
           "wk": wk.astype(BF16),
           "qg": jnp.pad(b_q_gain[j], (0, HP - B_QK))[None, :],
           "kg": jnp.pad(b_k_gain[j], (0, HP - B_QK))[None, :]}
    return w, rw, mla, w_out_even[j].astype(BF16)


def kernel(x, c, ctx, c_ctx, norm1_g, norm2_g, mod_w, mod_b, w_in, a_mu, a_w0, a_w2, a_a0, a_a2, a_g2,
           a_k_k, a_k_a, a_r_k, a_ln_w, a_ln_b, b_q_norm, b_w_q_up, b_kv_norm, b_w_kv_up, b_q_gain,
           b_k_gain, w_out_even, w_out_fourier, router_w, router_b, moe_w_gate, moe_w_up, moe_w_down,
           shared_w_gate, shared_w_up, shared_w_down):
    bsz, seq, d = x.shape
    ctx_len = ctx.shape[1]
    assert d == D and seq % TM == 0 and ctx_len % TM == 0 and seq % GRID_W == 0
    assert DEPTH % 2 == 0
    nct = ctx_len // TM
    xs = jnp.concatenate([ctx, x], axis=1)
    n_all = bsz * (ctx_len + seq)
    n_slots = (n_all + N_CLASS * (MB - 1) + MB - 1) // MB * MB
    xs_buf = jnp.zeros((n_slots, D + EXT), F32)

    rows = (bsz + 1 + 7) // 8 * 8
    cs = jnp.zeros((rows, D), F32).at[0:bsz].set(c).at[bsz].set(c_ctx)
    mod = _modulation(cs, mod_w, mod_b)

    def table(part):
        lat = part[0:bsz]
        cx = jnp.broadcast_to(part[bsz][None, :], (bsz, D))
        return jnp.stack([cx, lat], axis=1)[:, :, None, :]

    consts = _rwkv_consts()
    rope = _rope_tables(ctx_len, seq)
    cs_mat, dlat, dctx = _fourier_consts(ctx_len, seq)
    rt = _route_consts(router_w, router_b)
    nctx_chunks = ctx_len // CH

    for i in range(DEPTH):
        j = i // 2
        t0 = nct if i == DEPTH - 1 else 0
        sh1, sc1, g1, sh2, sc2, g2 = [mod[i][:, D * q:D * (q + 1)] for q in range(6)]
        t_sc1 = table((1.0 + sc1) * norm1_g[i][None, :])
        t_sh1 = table(sh1)
        t_g1 = table(g1)
        t_sc2 = table((1.0 + sc2) * norm2_g[i][None, :])
        t_sh2 = table(sh2)
        t_g2 = table(g2)
        if i % 2 == 0:
            w, rw, mla, wo = _even_params(
                j, w_in, a_mu, a_w0, a_w2, a_a0, a_a2, a_g2, a_k_k, a_k_a, a_r_k, a_ln_w, a_ln_b,
                b_q_norm, b_w_q_up, b_kv_norm, b_w_kv_up, b_q_gain, b_k_gain, w_out_even)
            pa, pb = _in_proj(xs, t_sc1, t_sh1, w, nct)
            yf = _rwkv_pass(pa, rw, consts, None, reverse=False, nctx=nctx_chunks)
            ya = _rwkv_pass(pa, rw, consts, yf, reverse=True, nctx=nctx_chunks)
            q, k, v = _mla_prep(pb, mla, rope)
            yb = _attention(q, k, v, ctx_len)
            xs, he, cnt = _even_out(ya, yb, wo, xs, t_g1, t_sc2, t_sh2, rt, nct)
        else:
            hc, hs = _fourier_channels(xs, t_sc1, t_sh1, cs_mat, nct)
            xs, he, cnt = _fourier_out(hc, hs, dlat, dctx, w_out_fourier[j].astype(BF16), xs,
                                       t_g1, t_sc2, t_sh2, rt, ctx_len, t0)
        xs, xs_buf = _moe_layer(
            xs, he, cnt, xs_buf, t_g2,
            moe_w_gate[i].astype(BF16), moe_w_up[i].astype(BF16), moe_w_down[i].astype(BF16),
            shared_w_gate[i].astype(BF16), shared_w_up[i].astype(BF16),
            shared_w_down[i].astype(BF16), nct, t0)
    return xs
```

```python
import functools
import math

import numpy as np
import jax
import jax.numpy as jnp
from jax import lax
from jax.experimental import pallas as pl
from jax.experimental.pallas import tpu as pltpu

F32 = jnp.float32
BF16 = jnp.bfloat16
HI = lax.Precision.HIGHEST

D = 1024
DEPTH = 4
GRID_W = 64
NORM_EPS = 1e-6
A_HEAD = 64
A_HEADS = 8
A_W = 512
A_RANK = 64
A_GATE = 128
A_DECAY_SCALE = 0.606531
A_LN_EPS = 64e-5
A_COLS = 3 * A_W + 4 * A_RANK + A_GATE
B_HEADS = 8
B_NOPE = 64
B_ROPE = 32
B_QK = 96
B_V = 64
B_Q_RANK = 256
B_KV_RANK = 128
B_COLS = B_Q_RANK + B_KV_RANK + B_ROPE
ATTN_SCALE = B_QK ** -0.5
LOG2E = math.log2(math.e)
ROPE_BASE = 10000.0
C_GROUPS = 8
C_GDIM = 128
N_EXPERTS = 16
N_GROUPS = 4
EPG = 4
D_EXPERT = 512
D_SHARED = 512

LANE = 128
TM = 256
CH = 64
RW_NB = 4
MB = 256
PERM_ROWS = 1024
PB_COLS = 512
HP = 128
N_CLASS = 24
EXT = 128
PAIRS = ((0, 1), (0, 2), (0, 3), (1, 2), (1, 3), (2, 3))
VMEM_LIMIT = 56 * 1024 * 1024


def _cparams(sem):
    return pltpu.CompilerParams(dimension_semantics=sem, vmem_limit_bytes=VMEM_LIMIT)


def _mm(a, b):
    return jnp.dot(a.astype(BF16), b.astype(BF16), preferred_element_type=F32)


def _mm_nt(a, b):
    return lax.dot_general(a.astype(BF16), b.astype(BF16), (((1,), (1,)), ((), ())),
                           preferred_element_type=F32)


def _mm_tn(a, b):
    return lax.dot_general(a.astype(BF16), b.astype(BF16), (((0,), (0,)), ((), ())),
                           preferred_element_type=F32)


def _mm_hi(a, b):
    return jnp.dot(a, b, precision=HI, preferred_element_type=F32)


def _split_bf16(x):
    hi = x.astype(BF16)
    return hi, (x - hi.astype(F32)).astype(BF16)


def _mm_split_rhs(a, b):
    hi, lo = _split_bf16(b)
    a = a.astype(BF16)
    return jnp.dot(a, hi, preferred_element_type=F32) + jnp.dot(a, lo, preferred_element_type=F32)


def _mm_split_lhs(a, b):
    hi, lo = _split_bf16(a)
    b = b.astype(BF16)
    return jnp.dot(hi, b, preferred_element_type=F32) + jnp.dot(lo, b, preferred_element_type=F32)


def _sigmoid(x):
    return 1.0 / (1.0 + jnp.exp(-x))


def _silu(x):
    return x * _sigmoid(x)


def _rms_mod(x, scale, shift):
    ms = jnp.mean(x * x, axis=-1, keepdims=True)
    return x * lax.rsqrt(ms + NORM_EPS) * scale + shift


def _mod_kernel(c_ref, w_ref, b_ref, o_ref):
    o_ref[...] = _mm_hi(_silu(c_ref[...]), w_ref[...]) + b_ref[...]


def _modulation(cs, mod_w, mod_b):
    rows = cs.shape[0]
    tn = 1536
    return pl.pallas_call(
        _mod_kernel,
        grid=(DEPTH, 6 * D // tn),
        in_specs=[pl.BlockSpec((rows, D), lambda l, n: (0, 0)),
                  pl.BlockSpec((None, D, tn), lambda l, n: (l, 0, n)),
                  pl.BlockSpec((None, 1, tn), lambda l, n: (l, 0, n))],
        out_specs=pl.BlockSpec((None, rows, tn), lambda l, n: (l, 0, n)),
        out_shape=jax.ShapeDtypeStruct((DEPTH, rows, 6 * D), F32),
        compiler_params=_cparams(("arbitrary", "arbitrary")),
        name="modulation",
    )(cs, mod_w, mod_b.reshape(DEPTH, 1, 6 * D))


def _tab_spec(nct, t0=0):
    return pl.BlockSpec((None, None, 1, D),
                        lambda b, t, *_: (b, jnp.minimum((t + t0) // nct, 1), 0, 0))


def _e1_kernel(x_ref, sc_ref, sh_ref, w_ref, pa_ref, pb_ref):
    h = _rms_mod(x_ref[...], sc_ref[...], sh_ref[...])
    p = _mm(h, w_ref[...])
    pa_ref[...] = p[:, :A_COLS]
    pb_ref[...] = p[:, A_COLS:]


def _in_proj(x, sc, sh, w, nct):
    bsz, s, _ = x.shape
    ncol = A_COLS + PB_COLS
    return pl.pallas_call(
        _e1_kernel,
        grid=(bsz, s // TM),
        in_specs=[pl.BlockSpec((None, TM, D), lambda b, t: (b, t, 0)),
                  _tab_spec(nct), _tab_spec(nct),
                  pl.BlockSpec((D, ncol), lambda b, t: (0, 0))],
        out_specs=[pl.BlockSpec((None, TM, A_COLS), lambda b, t: (b, t, 0)),
                   pl.BlockSpec((None, TM, PB_COLS), lambda b, t: (b, t, 0))],
        out_shape=[jax.ShapeDtypeStruct((bsz, s, A_COLS), F32),
                   jax.ShapeDtypeStruct((bsz, s, PB_COLS), F32)],
        compiler_params=_cparams(("parallel", "parallel")),
        name="in_proj",
    )(x, sc, sh, w)


def _chunk_of(i, nctx, nch, reverse):
    if not reverse:
        return i
    return jnp.where(i < nctx, nctx - 1 - i, nch - 1 - (i - nctx))


def _rw_kernel(*refs, reverse, final, nctx, nch, nb):
    if final:
        (pc_ref, pp_ref, pn_ref, mu_ref, w0_ref, w2_ref, a0_ref, a2_ref, kk_ref, ka_ref,
         tri_ref, ms_ref, mi_ref, bones_ref,
         a0o_ref, a2o_ref, g2_ref, rk_ref, lnw_ref, lnb_ref, yf_ref, out_ref, z_ref) = refs
    else:
        (pc_ref, pp_ref, pn_ref, mu_ref, w0_ref, w2_ref, a0_ref, a2_ref, kk_ref, ka_ref,
         tri_ref, ms_ref, mi_ref, bones_ref, out_ref, z_ref) = refs
    i = pl.program_id(1)
    c = _chunk_of(i, nctx, nch, reverse)

    @pl.when(i == 0)
    def _():
        z_ref[...] = jnp.zeros_like(z_ref)

    seg_first = jnp.logical_or(c == 0, c == nctx)
    seg_last = jnp.logical_or(c == nctx - 1, c == nch - 1)
    rid = lax.broadcasted_iota(jnp.int32, (CH, 1), 0)
    bones = bones_ref[...]
    lane = lax.broadcasted_iota(jnp.int32, (1, LANE), 1)
    m0 = (lane < A_HEAD).astype(F32)
    m1 = 1.0 - m0
    eye = (lax.broadcasted_iota(jnp.int32, (LANE, LANE), 0)
           == lax.broadcasted_iota(jnp.int32, (LANE, LANE), 1))
    eye_c = rid == (lane & (A_HEAD - 1))
    mstrict = ms_ref[...] > 0.5
    mincl = mi_ref[...] > 0.5
    h0 = lane < A_HEAD
    h0w = jnp.concatenate([h0, h0], axis=1)
    zero_b = jnp.zeros((), BF16)

    def stack(x):
        x = x.astype(BF16)
        sel = h0 if x.shape[1] == LANE else h0w
        return jnp.concatenate([jnp.where(sel, x, zero_b), jnp.where(sel, zero_b, x)], axis=0)

    pre = []
    chains = []
    for n in range(nb):
        p = pc_ref[n]
        prow = jnp.where(seg_first, 0.0, pp_ref[n, 7:8, :])
        nrow = jnp.where(seg_last, 0.0, pn_ref[n, 0:1, :])
        pprev = jnp.where(rid == 0, prow, pltpu.roll(p, 1, 0))
        pnext = jnp.where(rid == CH - 1, nrow, pltpu.roll(p, CH - 1, 0))
        ps = p + mu_ref[...] * (0.5 * (pprev + pnext) - p)
        r = ps[:, 0:A_W]
        k = ps[:, A_W:2 * A_W]
        v = ps[:, 2 * A_W:3 * A_W]
        dd = ps[:, 3 * A_W:3 * A_W + 128]
        ad = ps[:, 3 * A_W + 128:3 * A_W + 256]
        gd = ps[:, 3 * A_W + 256:3 * A_W + 384]
        logw = -A_DECAY_SCALE * _sigmoid(w0_ref[...] + _mm(jnp.tanh(dd), w2_ref[...]))
        aic = _sigmoid(a0_ref[...] + _mm(ad, a2_ref[...]))
        kkr = k * kk_ref[...]
        kk = kkr * lax.rsqrt(_mm_split_lhs(kkr * kkr, bones) + 1e-12)
        kd = k * (1.0 + (aic - 1.0) * ka_ref[...])
        bd = kk * aic
        cum = _mm_split_rhs(tri_ref[...], logw)
        tot = cum[0:1, :] if reverse else cum[CH - 1:CH, :]
        rt = r * jnp.exp(cum)
        at = -kk * jnp.exp(cum - logw)
        ei = jnp.exp(-cum)
        kt = kd * ei
        bt = bd * ei
        et = jnp.exp(tot - cum)
        kh = kd * et
        bh = bd * et
        gam = jnp.exp(tot)
        pre.append((r, k, v, ad, gd, kd))
        for j in range(A_HEADS // 2):
            sl = slice(LANE * j, LANE * (j + 1))
            chains.append({
                "n": n, "j": j, "gam": gam[:, sl], "a": at[:, sl], "r": rt[:, sl],
                "bts": stack(bt[:, sl]), "kts": stack(kt[:, sl]), "vs": stack(v[:, sl]),
                "bhs": stack(bh[:, sl]), "khs": stack(kh[:, sl])})

    for ch in chains:
        big = _mm_nt(jnp.concatenate([ch["a"], ch["r"]], axis=0),
                     jnp.concatenate([ch["bts"], ch["kts"]], axis=0))
        ch["nm"] = jnp.where(mstrict, big[0:CH, 0:128], 0.0)
        aak = jnp.where(mstrict, big[0:CH, 128:256], 0.0)
        ch["arb"] = jnp.where(mincl, big[CH:2 * CH, 0:128], 0.0)
        ark = jnp.where(mincl, big[CH:2 * CH, 128:256], 0.0)
        ch["aa"] = jnp.concatenate([aak, ark], axis=0)
    for ch in chains:
        avs = _mm(ch["aa"], ch["vs"])
        ch["rhs"] = jnp.concatenate([ch["a"], avs[0:CH]], axis=1)
        ch["av2"] = avs[CH:2 * CH]
        ch["t"] = jnp.where(eye_c, 1.0, ch["nm"])
    for it in range(5):
        for ch in chains:
            n2 = _mm(ch["nm"], stack(ch["nm"]))
            ch["t"] = ch["t"] + _mm(n2, stack(ch["t"]))
            ch["nm"] = n2
    for ch in chains:
        ch["x"] = stack(_mm(ch["t"], stack(ch["rhs"])))
    ys = [[None] * (A_HEADS // 2) for _ in range(nb)]
    for ch in chains:
        x = ch["x"]
        qy = jnp.concatenate([ch["r"], ch["av2"]], axis=1) + _mm(ch["arb"], x)
        z = z_ref[ch["n"], ch["j"]]
        ys[ch["n"]][ch["j"]] = qy[:, 128:256] + _mm(qy[:, 0:128], z)
        lhs = jnp.concatenate([ch["bhs"], ch["khs"]], axis=0)
        rhs = jnp.concatenate(
            [x, jnp.concatenate([jnp.zeros((128, 128), BF16), ch["vs"]], axis=1)], axis=0)
        m = _mm_tn(lhs, rhs)
        gt = jnp.where(eye, ch["gam"], 0.0) + m[:, 0:128]
        z_ref[ch["n"], ch["j"]] = _mm(gt, z) + m[:, 128:256]

    for n in range(nb):
        y = jnp.concatenate(ys[n], axis=1)
        if not final:
            out_ref[n] = y
        else:
            r, k, v, ad, gd, kd = pre[n]
            y = y + yf_ref[n]
            mean = _mm_split_lhs(y, bones) * (1.0 / A_HEAD)
            yc = y - mean
            var = _mm_split_lhs(yc * yc, bones) * (1.0 / A_HEAD)
            yn = yc * lax.rsqrt(var + A_LN_EPS) * lnw_ref[...] + lnb_ref[...]
            aico = _sigmoid(a0o_ref[...] + _mm(ad, a2o_ref[...]))
            kdo = k * (1.0 + (aico - 1.0) * ka_ref[...])
            bonus = _mm_split_lhs(r * rk_ref[...] * (kd + kdo), bones) * v
            g = _mm(_sigmoid(gd), g2_ref[...])
            out_ref[n] = ((yn + bonus) * g).astype(out_ref.dtype)


def _rwkv_pass(pa, prm, consts, yf, *, reverse, nctx):
    bsz, s, _ = pa.shape
    nch = s // CH
    final = yf is not None
    d = 1 if reverse else 0
    nb = RW_NB
    assert bsz % nb == 0

    def cmap(b, i):
        return (b, _chunk_of(i, nctx, nch, reverse), 0)

    def pmap(b, i):
        return (b, jnp.maximum(_chunk_of(i, nctx, nch, reverse) * (CH // 8) - 1, 0), 0)

    def nmap(b, i):
        return (b, jnp.minimum((_chunk_of(i, nctx, nch, reverse) + 1) * (CH // 8), s // 8 - 1), 0)

    def full(a):
        return pl.BlockSpec(a.shape, lambda b, i: (0,) * a.ndim)

    args = [pa, pa, pa, prm["mu"], prm["w0"][d], prm["w2"][d], prm["a0"][d], prm["a2"][d],
            prm["kk"], prm["ka"], consts["tri"][d], consts["mstrict"][d], consts["mincl"][d],
            consts["bones"]]
    specs = [pl.BlockSpec((nb, CH, A_COLS), cmap),
             pl.BlockSpec((nb, 8, A_COLS), pmap),
             pl.BlockSpec((nb, 8, A_COLS), nmap)] + [full(a) for a in args[3:]]
    if final:
        extra = [prm["a0"][1 - d], prm["a2"][1 - d], prm["g2"], prm["rk"], prm["lnw"], prm["lnb"]]
        args += extra + [yf]
        specs += [full(a) for a in extra] + [pl.BlockSpec((nb, CH, A_W), cmap)]
    return pl.pallas_call(
        functools.partial(_rw_kernel, reverse=reverse, final=final, nctx=nctx, nch=nch, nb=nb),
        grid=(bsz // nb, nch),
        in_specs=specs,
        out_specs=pl.BlockSpec((nb, CH, A_W), cmap),
        out_shape=jax.ShapeDtypeStruct((bsz, s, A_W), BF16 if final else F32),
        scratch_shapes=[pltpu.VMEM((nb, A_HEADS // 2, LANE, LANE), F32)],
        compiler_params=_cparams(("parallel", "arbitrary")),
        name="rwkv_bwd" if reverse else "rwkv_fwd",
    )(*args)


def _mla_prep_kernel(pb_ref, qn_ref, kvn_ref, wq_ref, wk_ref, qg_ref, kg_ref, rp_ref,
                     c_ref, s1_ref, q_ref, k_ref, v_ref):
    pb = pb_ref[...]
    pq = pb[:, 0:B_Q_RANK]
    xq = pq * lax.rsqrt(jnp.mean(pq * pq, axis=-1, keepdims=True) + NORM_EPS) * qn_ref[...]
    q = _mm(xq, wq_ref[...])
    pk = pb[:, B_Q_RANK:PB_COLS]
    is_kv = lax.broadcasted_iota(jnp.int32, (1, PB_COLS - B_Q_RANK), 1) < B_KV_RANK
    ms = jnp.sum(jnp.where(is_kv, pk * pk, 0.0), axis=-1, keepdims=True) * (1.0 / B_KV_RANK)
    xk = jnp.where(is_kv, pk * lax.rsqrt(ms + NORM_EPS) * kvn_ref[...], pk)
    kv = _mm(xk, wk_ref[...])
    cc, s1 = c_ref[...], s1_ref[...]

    def head(xh, gain):
        ss = jnp.sum(xh * xh, axis=-1, keepdims=True) * (1.0 / B_QK)
        xh = xh * lax.rsqrt(ss + NORM_EPS) * gain
        return xh * cc + _mm(xh, rp_ref[...]) * s1

    for h in range(B_HEADS):
        sl = slice(HP * h, HP * (h + 1))
        q_ref[:, sl] = (head(q[:, sl], qg_ref[...]) * (ATTN_SCALE * LOG2E)).astype(BF16)
        k_ref[:, sl] = head(kv[:, sl], kg_ref[...]).astype(BF16)
    v_ref[...] = kv[:, B_HEADS * HP:].astype(BF16)


def _mla_prep(pb, prm, rope):
    bsz, s, _ = pb.shape

    def full(a):
        return pl.BlockSpec(a.shape, lambda b, t: (0,) * a.ndim)

    consts = [prm["qn"], prm["kvn"], prm["wq"], prm["wk"], prm["qg"], prm["kg"], rope[0]]
    rspec = pl.BlockSpec((TM, HP), lambda b, t: (t, 0))
    return pl.pallas_call(
        _mla_prep_kernel,
        grid=(bsz, s // TM),
        in_specs=[pl.BlockSpec((None, TM, PB_COLS), lambda b, t: (b, t, 0))]
        + [full(a) for a in consts] + [rspec, rspec],
        out_specs=[pl.BlockSpec((None, TM, B_HEADS * HP), lambda b, t: (b, t, 0)),
                   pl.BlockSpec((None, TM, B_HEADS * HP), lambda b, t: (b, t, 0)),
                   pl.BlockSpec((None, TM, B_HEADS * B_V), lambda b, t: (b, t, 0))],
        out_shape=[jax.ShapeDtypeStruct((bsz, s, B_HEADS * HP), BF16),
                   jax.ShapeDtypeStruct((bsz, s, B_HEADS * HP), BF16),
                   jax.ShapeDtypeStruct((bsz, s, B_HEADS * B_V), BF16)],
        compiler_params=_cparams(("parallel", "parallel")),
        name="mla_prep",
    )(pb, *consts, *rope[1:])


def _att_kernel(q_ref, k_ref, v_ref, o_ref, *, ctx_len, nct):
    t = pl.program_id(2)
    lane = lax.broadcasted_iota(jnp.int32, (1, 2 * B_V), 1)

    def run(nk):
        outs = []
        for s in range(2):
            q = q_ref[:, HP * s:HP * (s + 1)]
            k = k_ref[0:nk, HP * s:HP * (s + 1)]
            sc = lax.dot_general(q, k, (((1,), (1,)), ((), ())), preferred_element_type=F32)
            p = jnp.exp2(sc - jnp.max(sc, axis=-1, keepdims=True)).astype(BF16)
            ones_lane = B_V * (1 - s)
            vm = jnp.where((lane >= B_V * s) & (lane < B_V * (s + 1)), v_ref[0:nk, :],
                           jnp.where(lane == ones_lane, 1.0, 0.0).astype(BF16))
            acc = jnp.dot(p, vm, preferred_element_type=F32)
            outs.append(acc / acc[:, ones_lane:ones_lane + 1])
        o_ref[...] = jnp.where(lane < B_V, outs[0], outs[1]).astype(BF16)

    @pl.when(t < nct)
    def _():
        run(ctx_len)

    @pl.when(t >= nct)
    def _():
        run(k_ref.shape[0])


def _attention(q, k, v, ctx_len):
    bsz, s, _ = q.shape
    nct = ctx_len // TM
    return pl.pallas_call(
        functools.partial(_att_kernel, ctx_len=ctx_len, nct=nct),
        grid=(bsz, B_HEADS // 2, s // TM),
        in_specs=[pl.BlockSpec((None, TM, 2 * HP), lambda b, j, t: (b, t, j)),
                  pl.BlockSpec((None, s, 2 * HP), lambda b, j, t: (b, 0, j)),
                  pl.BlockSpec((None, s, 2 * B_V), lambda b, j, t: (b, 0, j))],
        out_specs=pl.BlockSpec((None, TM, 2 * B_V), lambda b, j, t: (b, t, j)),
        out_shape=jax.ShapeDtypeStruct((bsz, s, B_HEADS * B_V), BF16),
        compiler_params=_cparams(("parallel", "parallel", "parallel")),
        name="mla_attention",
    )(q, k, v)


def _route_tail(x, o, g1, sc2, sh2, rwh_ref, rwl_ref, rb_ref, pm_ref, pab_ref, stril_ref,
                xo_ref, he_ref, cnt_ref, cnt_scr, first):
    xn = x + g1 * o
    xo_ref[...] = xn
    h2 = _rms_mod(xn, sc2, sh2)
    he_ref[:, 0:D] = h2

    @pl.when(first)
    def _():
        cnt_scr[...] = jnp.zeros_like(cnt_scr)

    lane = lax.broadcasted_iota(jnp.int32, (1, LANE), 1)
    h_hi, h_lo = _split_bf16(h2)
    r_hi, r_lo = rwh_ref[...], rwl_ref[...]
    logits = (jnp.dot(h_hi, r_hi, preferred_element_type=F32)
              + jnp.dot(h_lo, r_hi, preferred_element_type=F32)
              + jnp.dot(h_hi, r_lo, preferred_element_type=F32))
    scores = _sigmoid(logits)
    sel = scores + rb_ref[...]
    ps = jnp.where(lane < N_CLASS, _mm_split_lhs(sel, pm_ref[...]), -jnp.inf)
    best = jnp.max(ps, axis=-1, keepdims=True)
    cls = jnp.min(jnp.where(ps == best, lane, LANE), axis=-1, keepdims=True)
    onehot = (lane == cls).astype(F32)
    s_hi, s_lo = _split_bf16(scores)
    sab = (jnp.dot(s_hi, pab_ref[...], preferred_element_type=F32)
           + jnp.dot(s_lo, pab_ref[...], preferred_element_type=F32))
    sa = jnp.sum(onehot * sab[:, 0:LANE], axis=-1, keepdims=True)
    sb = jnp.sum(onehot * sab[:, LANE:2 * LANE], axis=-1, keepdims=True)
    den = sa + sb
    prefix = jnp.dot(stril_ref[...], onehot.astype(BF16), preferred_element_type=F32)
    rank = jnp.sum(onehot * (prefix + cnt_scr[0:1, :]), axis=-1, keepdims=True)
    cnt_scr[...] = cnt_scr[...] + jnp.sum(onehot, axis=0, keepdims=True)
    cnt_ref[...] = cnt_scr[...]
    he_ref[:, D:D + EXT] = jnp.where(
        lane == 0, sa / den,
        jnp.where(lane == 1, sb / den,
                  jnp.where(lane == 2, cls.astype(F32), jnp.where(lane == 3, rank, 0.0))))


def _tail_out(bsz, s_out):
    specs = [pl.BlockSpec((None, TM, D), lambda b, t: (b, t, 0)),
             pl.BlockSpec((None, TM, D + EXT), lambda b, t: (b, t, 0)),
             pl.BlockSpec((8, LANE), lambda b, t: (0, 0))]
    shapes = [jax.ShapeDtypeStruct((bsz, s_out, D), F32),
              jax.ShapeDtypeStruct((bsz, s_out, D + EXT), F32),
              jax.ShapeDtypeStruct((8, LANE), F32)]
    return specs, shapes


def _e2_kernel(ya_ref, yb_ref, wo_ref, x_ref, g1_ref, sc_ref, sh_ref, rwh_ref, rwl_ref, rb_ref, pm_ref,
               pab_ref, stril_ref, xo_ref, he_ref, cnt_ref, cnt_scr):
    o = (jnp.dot(ya_ref[...], wo_ref[0:A_W, :], preferred_element_type=F32)
         + jnp.dot(yb_ref[...], wo_ref[A_W:2 * A_W, :], preferred_element_type=F32))
    first = jnp.logical_and(pl.program_id(0) == 0, pl.program_id(1) == 0)
    _route_tail(x_ref[...], o, g1_ref[...], sc_ref[...], sh_ref[...], rwh_ref, rwl_ref, rb_ref, pm_ref,
                pab_ref, stril_ref, xo_ref, he_ref, cnt_ref, cnt_scr, first)


def _even_out(ya, yb, wo, x, g1, sc2, sh2, rt, nct):
    bsz, s, _ = x.shape
    out_specs, out_shape = _tail_out(bsz, s)

    def full(a):
        return pl.BlockSpec(a.shape, lambda b, t: (0,) * a.ndim)

    return pl.pallas_call(
        _e2_kernel,
        grid=(bsz, s // TM),
        in_specs=[pl.BlockSpec((None, TM, A_W), lambda b, t: (b, t, 0)),
                  pl.BlockSpec((None, TM, A_W), lambda b, t: (b, t, 0)),
                  full(wo),
                  pl.BlockSpec((None, TM, D), lambda b, t: (b, t, 0)),
                  _tab_spec(nct), _tab_spec(nct), _tab_spec(nct)] + [full(a) for a in rt],
        out_specs=out_specs,
        out_shape=out_shape,
        scratch_shapes=[pltpu.VMEM((8, LANE), F32)],
        compiler_params=_cparams(("arbitrary", "arbitrary")),
        name="even_out_route",
    )(ya, yb, wo, x, g1, sc2, sh2, *rt)


def _f1_kernel(x_ref, sc_ref, sh_ref, cs_ref, hc_ref, hs_ref):
    h = _rms_mod(x_ref[...], sc_ref[...], sh_ref[...]).astype(BF16)
    for g in range(C_GROUPS):
        sl = slice(C_GDIM * g, C_GDIM * (g + 1))
        r = jnp.dot(h[:, sl], cs_ref[...], preferred_element_type=F32)
        hc_ref[:, sl] = r[:, 0:C_GDIM].astype(BF16)
        hs_ref[:, sl] = r[:, C_GDIM:2 * C_GDIM].astype(BF16)


def _fourier_channels(x, sc, sh, cs, nct):
    bsz, s, _ = x.shape
    return pl.pallas_call(
        _f1_kernel,
        grid=(bsz, s // TM),
        in_specs=[pl.BlockSpec((None, TM, D), lambda b, t: (b, t, 0)),
                  _tab_spec(nct), _tab_spec(nct),
                  pl.BlockSpec(cs.shape, lambda b, t: (0, 0))],
        out_specs=[pl.BlockSpec((None, TM, D), lambda b, t: (b, t, 0)),
                   pl.BlockSpec((None, TM, D), lambda b, t: (b, t, 0))],
        out_shape=[jax.ShapeDtypeStruct((bsz, s, D), BF16),
                   jax.ShapeDtypeStruct((bsz, s, D), BF16)],
        compiler_params=_cparams(("parallel", "parallel")),
        name="fourier_channels",
    )(x, sc, sh, cs)


def _f2_kernel(hc_ref, hs_ref, dl_ref, dc_ref, wo_ref, x_ref, g1_ref, sc_ref, sh_ref, rwh_ref, rwl_ref,
               rb_ref, pm_ref, pab_ref, stril_ref, xo_ref, he_ref, cnt_ref, cnt_scr, f_scr,
               *, ctx_len, nct, t0):
    t = pl.program_id(1) + t0
    s = hc_ref.shape[0]
    seq = s - ctx_len

    if t0 < nct:
        @pl.when(t < nct)
        def _():
            f_scr[...] = (jnp.dot(dc_ref[:, 0:ctx_len], hc_ref[0:ctx_len, :], preferred_element_type=F32)
                          + jnp.dot(dc_ref[:, ctx_len:2 * ctx_len], hs_ref[0:ctx_len, :],
                                    preferred_element_type=F32))

    @pl.when(t >= nct)
    def _():
        f_scr[...] = (jnp.dot(dl_ref[:, 0:seq], hc_ref[ctx_len:s, :], preferred_element_type=F32)
                      + jnp.dot(dl_ref[:, seq:2 * seq], hs_ref[ctx_len:s, :],
                                preferred_element_type=F32))

    o = _mm(f_scr[...], wo_ref[...])
    first = jnp.logical_and(pl.program_id(0) == 0, pl.program_id(1) == 0)
    _route_tail(x_ref[...], o, g1_ref[...], sc_ref[...], sh_ref[...], rwh_ref, rwl_ref, rb_ref, pm_ref,
                pab_ref, stril_ref, xo_ref, he_ref, cnt_ref, cnt_scr, first)


def _fourier_out(hc, hs, dlat, dctx, wo, x, g1, sc2, sh2, rt, ctx_len, t0):
    bsz, s, _ = x.shape
    nct = ctx_len // TM
    nt = s // TM - t0
    out_specs, out_shape = _tail_out(bsz, nt * TM)

    def full(a):
        return pl.BlockSpec(a.shape, lambda b, t: (0,) * a.ndim)

    return pl.pallas_call(
        functools.partial(_f2_kernel, ctx_len=ctx_len, nct=nct, t0=t0),
        grid=(bsz, nt),
        in_specs=[pl.BlockSpec((None, s, D), lambda b, t: (b, 0, 0)),
                  pl.BlockSpec((None, s, D), lambda b, t: (b, 0, 0)),
                  pl.BlockSpec((TM, dlat.shape[1]), lambda b, t: (jnp.maximum(t + t0 - nct, 0), 0)),
                  pl.BlockSpec((TM, dctx.shape[1]), lambda b, t: (jnp.minimum(t + t0, nct - 1), 0)),
                  full(wo),
                  pl.BlockSpec((None, TM, D), lambda b, t: (b, t + t0, 0)),
                  _tab_spec(nct, t0), _tab_spec(nct, t0), _tab_spec(nct, t0)]
        + [full(a) for a in rt],
        out_specs=out_specs,
        out_shape=out_shape,
        scratch_shapes=[pltpu.VMEM((8, LANE), F32), pltpu.VMEM((TM, D), F32)],
        compiler_params=_cparams(("arbitrary", "arbitrary")),
        name="fourier_out_route",
    )(hc, hs, dlat, dctx, wo, x, g1, sc2, sh2, *rt)


def _row_copy(src_ref, src_row, dst_ref, dst_row, sem):
    return pltpu.make_async_copy(src_ref.at[pl.ds(src_row, 1)], dst_ref.at[pl.ds(dst_row, 1)], sem)


def _scatter_kernel(slot_ref, h_ref, xin_ref, xs_ref, sem, *, rows):
    del xin_ref
    base = pl.program_id(0) * rows

    def issue(r, carry):
        _row_copy(h_ref, r, xs_ref, slot_ref[base + r], sem).start()
        return carry

    lax.fori_loop(0, rows, issue, 0, unroll=8)
    pltpu.make_async_copy(h_ref, xs_ref.at[pl.ds(0, rows)], sem).wait()


def _scatter_rows(slot, src, dst):
    n = src.shape[0]
    rows = math.gcd(n, PERM_ROWS)
    return pl.pallas_call(
        functools.partial(_scatter_kernel, rows=rows),
        grid_spec=pltpu.PrefetchScalarGridSpec(
            num_scalar_prefetch=1,
            grid=(n // rows,),
            in_specs=[pl.BlockSpec((rows, src.shape[1]), lambda i, sl: (i, 0)),
                      pl.BlockSpec(memory_space=pl.ANY)],
            out_specs=pl.BlockSpec(memory_space=pl.ANY),
            scratch_shapes=[pltpu.SemaphoreType.DMA(())]),
        out_shape=jax.ShapeDtypeStruct(dst.shape, dst.dtype),
        input_output_aliases={2: 0},
        compiler_params=_cparams(("arbitrary",)),
        name="moe_scatter_rows",
    )(slot, src, dst)


def _moe_kernel(ea_ref, eb_ref, ok_ref, xs_ref, wga_ref, wgb_ref, wua_ref, wub_ref, wda_ref, wdb_ref,
                y_ref):
    i = pl.program_id(0)

    @pl.when(ok_ref[i] > 0)
    def _():
        xe = xs_ref[...]
        x = xe[:, 0:D].astype(BF16)
        wa = xe[:, D:D + 1]
        wb = xe[:, D + 1:D + 2]
        ha = _silu(jnp.dot(x, wga_ref[...], preferred_element_type=F32)) \
            * jnp.dot(x, wua_ref[...], preferred_element_type=F32) * wa
        hb = _silu(jnp.dot(x, wgb_ref[...], preferred_element_type=F32)) \
            * jnp.dot(x, wub_ref[...], preferred_element_type=F32) * wb
        y_ref[...] = (jnp.dot(ha.astype(BF16), wda_ref[...], preferred_element_type=F32)
                      + jnp.dot(hb.astype(BF16), wdb_ref[...], preferred_element_type=F32))

    @pl.when(ok_ref[i] == 0)
    def _():
        y_ref[...] = jnp.zeros_like(y_ref)


def _moe_blocks(ea, eb, ok, xs, wg, wu, wd):
    n_slots = xs.shape[0]
    wa_spec = pl.BlockSpec((None, D, D_EXPERT), lambda i, ea, eb, ok: (ea[i], 0, 0))
    wb_spec = pl.BlockSpec((None, D, D_EXPERT), lambda i, ea, eb, ok: (eb[i], 0, 0))
    da_spec = pl.BlockSpec((None, D_EXPERT, D), lambda i, ea, eb, ok: (ea[i], 0, 0))
    db_spec = pl.BlockSpec((None, D_EXPERT, D), lambda i, ea, eb, ok: (eb[i], 0, 0))
    return pl.pallas_call(
        _moe_kernel,
        grid_spec=pltpu.PrefetchScalarGridSpec(
            num_scalar_prefetch=3,
            grid=(n_slots // MB,),
            in_specs=[pl.BlockSpec((MB, D + EXT), lambda i, ea, eb, ok: (i, 0)),
                      wa_spec, wb_spec, wa_spec, wb_spec, da_spec, db_spec],
            out_specs=pl.BlockSpec((MB, D), lambda i, ea, eb, ok: (i, 0))),
        out_shape=jax.ShapeDtypeStruct((n_slots, D), F32),
        compiler_params=_cparams(("arbitrary",)),
        name="moe_experts",
    )(ea, eb, ok, xs, wg, wg, wu, wu, wd, wd)


def _comb_kernel(slot_ref, x_ref, he_ref, g2_ref, wg_ref, wu_ref, wd_ref, y_ref, o_ref, ybuf, sem,
                 *, nt, nsteps):
    step = pl.program_id(0) * nt + pl.program_id(1)
    cur = step % 2

    def gather(tile, buf):
        base = tile * TM

        def issue(r, carry):
            _row_copy(y_ref, slot_ref[base + r], ybuf.at[buf], r, sem.at[buf]).start()
            return carry

        lax.fori_loop(0, TM, issue, 0, unroll=8)

    @pl.when(step == 0)
    def _():
        gather(0, 0)

    @pl.when(step + 1 < nsteps)
    def _():
        gather(step + 1, 1 - cur)

    h = he_ref[:, 0:D].astype(BF16)
    hid = _silu(jnp.dot(h, wg_ref[...], preferred_element_type=F32)) \
        * jnp.dot(h, wu_ref[...], preferred_element_type=F32)
    shared = jnp.dot(hid.astype(BF16), wd_ref[...], preferred_element_type=F32)
    pltpu.make_async_copy(y_ref.at[pl.ds(0, TM)], ybuf.at[cur], sem.at[cur]).wait()
    o_ref[...] = x_ref[...] + g2_ref[...] * (shared + ybuf[cur])


def _combine(slot, x, he, y, g2, wg, wu, wd, nct, t0):
    bsz, s, _ = x.shape
    nt = s // TM

    def full(a):
        return pl.BlockSpec(a.shape, lambda b, t, sl: (0,) * a.ndim)

    return pl.pallas_call(
        functools.partial(_comb_kernel, nt=nt, nsteps=bsz * nt),
        grid_spec=pltpu.PrefetchScalarGridSpec(
            num_scalar_prefetch=1,
            grid=(bsz, nt),
            in_specs=[pl.BlockSpec((None, TM, D), lambda b, t, sl: (b, t, 0)),
                      pl.BlockSpec((None, TM, D + EXT), lambda b, t, sl: (b, t, 0)),
                      _tab_spec(nct, t0), full(wg), full(wu), full(wd),
                      pl.BlockSpec(memory_space=pl.ANY)],
            out_specs=pl.BlockSpec((None, TM, D), lambda b, t, sl: (b, t, 0)),
            scratch_shapes=[pltpu.VMEM((2, TM, D), F32), pltpu.SemaphoreType.DMA((2,))]),
        out_shape=jax.ShapeDtypeStruct((bsz, s, D), F32),
        compiler_params=_cparams(("arbitrary", "arbitrary")),
        name="moe_combine",
    )(slot, x, he, g2, wg, wu, wd, y)


def _moe_layer(x, he, cnt, xs_buf, g2, wg, wu, wd, swg, swu, swd, nct, t0):
    bsz, s, _ = x.shape
    n = bsz * s
    n_slots = xs_buf.shape[0]
    cls = he[:, :, D + 2].reshape(n).astype(jnp.int32)
    rank = he[:, :, D + 3].reshape(n).astype(jnp.int32)
    counts = cnt[0, 0:N_CLASS].astype(jnp.int32)
    padded = (counts + MB - 1) // MB * MB
    pend = jnp.cumsum(padded)
    slot = (pend - padded)[cls] + rank
    bstart = jnp.arange(n_slots // MB, dtype=jnp.int32) * MB
    bcls = jnp.minimum(jnp.sum((bstart[:, None] >= pend[None, :]).astype(jnp.int32), axis=1),
                       N_CLASS - 1)
    ok = (bstart < pend[-1]).astype(jnp.int32)
    pa = jnp.asarray([p[0] for p in PAIRS], jnp.int32)
    pb = jnp.asarray([p[1] for p in PAIRS], jnp.int32)
    ea = EPG * (bcls // len(PAIRS)) + pa[bcls % len(PAIRS)]
    eb = EPG * (bcls // len(PAIRS)) + pb[bcls % len(PAIRS)]
    xs_buf = _scatter_rows(slot, he.reshape(n, D + EXT), xs_buf)
    y = _moe_blocks(ea, eb, ok, xs_buf, wg, wu, wd)
    return _combine(slot, x, he, y, g2, swg, swu, swd, nct, t0), xs_buf


def _rwkv_consts():
    t = np.arange(CH)
    low = (t[:, None] >= t[None, :]).astype(np.float32)
    tri = np.stack([low, low.T])
    t2 = np.arange(2 * CH) % CH
    mstrict = np.stack([t[:, None] > t2[None, :], t[:, None] < t2[None, :]]).astype(np.float32)
    mincl = np.stack([t[:, None] >= t2[None, :], t[:, None] <= t2[None, :]]).astype(np.float32)
    bones = np.kron(np.eye(A_HEADS, dtype=np.float32), np.ones((A_HEAD, A_HEAD), np.float32))
    return {"tri": jnp.asarray(tri), "mstrict": jnp.asarray(mstrict), "mincl": jnp.asarray(mincl),
            "bones": jnp.asarray(bones)}


def _route_consts(router_w, router_b):
    pm = np.zeros((LANE, LANE), np.float32)
    pa = np.zeros((LANE, LANE), np.float32)
    pb = np.zeros((LANE, LANE), np.float32)
    for cidx in range(N_CLASS):
        g, (a, b) = cidx // len(PAIRS), PAIRS[cidx % len(PAIRS)]
        pm[EPG * g + a, cidx] = 1.0
        pm[EPG * g + b, cidx] = 1.0
        pa[EPG * g + a, cidx] = 1.0
        pb[EPG * g + b, cidx] = 1.0
    tt = np.arange(TM)
    stril = (tt[:, None] > tt[None, :]).astype(np.float32)
    rw = jnp.pad(router_w, ((0, 0), (0, LANE - N_EXPERTS)))
    rw_hi = rw.astype(BF16)
    rw_lo = (rw - rw_hi.astype(F32)).astype(BF16)
    rb = jnp.pad(router_b, (0, LANE - N_EXPERTS))[None, :]
    pab = np.concatenate([pa, pb], axis=1)
    return [rw_hi, rw_lo, rb, jnp.asarray(pm, BF16), jnp.asarray(pab, BF16), jnp.asarray(stril, BF16)]


def _rope_tables(ctx_len, seq):
    rows = seq // GRID_W
    row = np.repeat(np.arange(rows), GRID_W).astype(np.float32)
    col = np.tile(np.arange(GRID_W), rows).astype(np.float32)
    half = B_ROPE // 2
    inv = (ROPE_BASE ** (-np.arange(0, half, 2, dtype=np.float32) / half)).astype(np.float32)
    ang = np.concatenate([row[:, None] * inv, col[:, None] * inv], axis=-1).astype(np.float32)
    cos = np.concatenate([np.ones((ctx_len, half), np.float32), np.cos(ang)], axis=0)
    sin = np.concatenate([np.zeros((ctx_len, half), np.float32), np.sin(ang)], axis=0)
    s = ctx_len + seq
    cc = np.zeros((s, HP), np.float32)
    ss = np.zeros((s, HP), np.float32)
    cc[:, 0:B_NOPE] = 1.0
    cc[:, B_NOPE:B_NOPE + half] = cos
    cc[:, B_NOPE + half:B_QK] = cos
    ss[:, B_NOPE:B_NOPE + half] = -sin
    ss[:, B_NOPE + half:B_QK] = sin
    partner = np.zeros((HP, HP), np.float32)
    i = np.arange(half)
    partner[B_NOPE + half + i, B_NOPE + i] = 1.0
    partner[B_NOPE + i, B_NOPE + half + i] = 1.0
    return jnp.asarray(partner, BF16), jnp.asarray(cc), jnp.asarray(ss)


def _dft_rows(n):
    idx = np.arange(n, dtype=np.int64)
    ang = 2.0 * np.pi * ((idx[:, None] * idx[None, :]) % n).astype(np.float64) / n
    scale = 1.0 / np.sqrt(n)
    return np.cos(ang) * scale, np.sin(ang) * scale


def _fourier_consts(ctx_len, seq):
    cc, sc = _dft_rows(C_GDIM)
    cs = np.concatenate([cc, sc], axis=1)
    cl, sl = _dft_rows(seq)
    cx, sx = _dft_rows(ctx_len)
    return (jnp.asarray(cs, BF16), jnp.asarray(np.concatenate([cl, -sl], axis=1), BF16),
            jnp.asarray(np.concatenate([cx, -sx], axis=1), BF16))


def _pad_heads(w, width, pad_to):
    kdim = w.shape[0]
    w = w.reshape(kdim, -1, width)
    w = jnp.pad(w, ((0, 0), (0, 0), (0, pad_to - width)))
    return w.reshape(kdim, -1)


def _even_params(j, w_in, a_mu, a_w0, a_w2, a_a0, a_a2, a_g2, a_k_k, a_k_a, a_r_k, a_ln_w, a_ln_b,
                 b_q_norm, b_w_q_up, b_kv_norm, b_w_kv_up, b_q_gain, b_k_gain, w_out_even):
    w = jnp.pad(w_in[j], ((0, 0), (0, PB_COLS - B_COLS))).astype(BF16)

    def lora_pad(m):
        z = jnp.zeros_like(m[0])
        return jnp.stack([jnp.concatenate([m[0], z], axis=0), jnp.concatenate([z, m[1]], axis=0)])

    rw = {"mu": a_mu[j][None, :],
          "w0": a_w0[j][:, None, :], "w2": lora_pad(a_w2[j]),
          "a0": a_a0[j][:, None, :], "a2": lora_pad(a_a2[j]),
          "g2": a_g2[j], "kk": a_k_k[j][None, :], "ka": a_k_a[j][None, :],
          "rk": a_r_k[j].reshape(1, A_W), "lnw": a_ln_w[j][None, :], "lnb": a_ln_b[j][None, :]}
    kvu = b_w_kv_up[j].reshape(B_KV_RANK, B_HEADS, B_NOPE + B_V)
    wk_nope = _pad_heads(kvu[:, :, :B_NOPE].reshape(B_KV_RANK, -1), B_NOPE, HP)
    wv = kvu[:, :, B_NOPE:].reshape(B_KV_RANK, -1)
    sel = np.zeros((B_ROPE, B_HEADS, HP), np.float32)
    for h in range(B_HEADS):
        sel[np.arange(B_ROPE), h, B_NOPE + np.arange(B_ROPE)] = 1.0
    krows = PB_COLS - B_Q_RANK
    wk = jnp.zeros((krows, B_HEADS * HP + B_HEADS * B_V), F32)
    wk = wk.at[0:B_KV_RANK, 0:B_HEADS * HP].set(wk_nope)
    wk = wk.at[B_KV_RANK:B_KV_RANK + B_ROPE, 0:B_HEADS * HP].set(jnp.asarray(sel.reshape(B_ROPE, -1)))
    wk = wk.at[0:B_KV_RANK, B_HEADS * HP:].set(wv)
    mla = {"qn": b_q_norm[j][None, :],
           "kvn": jnp.pad(b_kv_norm[j], (0, krows - B_KV_RANK), constant_values=1.0)[None, :],
           "wq": _pad_heads(b_w_q_up[j], B_QK, HP).astype(BF16),
           "wk": wk.astype(BF16),
           "qg": jnp.pad(b_q_gain[j], (0, HP - B_QK))[None, :],
           "kg": jnp.pad(b_k_gain[j], (0, HP - B_QK))[None, :]}
    return w, rw, mla, w_out_even[j].astype(BF16)


def kernel(x, c, ctx, c_ctx, norm1_g, norm2_g, mod_w, mod_b, w_in, a_mu, a_w0, a_w2, a_a0, a_a2, a_g2,
           a_k_k, a_k_a, a_r_k, a_ln_w, a_ln_b, b_q_norm, b_w_q_up, b_kv_norm, b_w_kv_up, b_q_gain,
           b_k_gain, w_out_even, w_out_fourier, router_w, router_b, moe_w_gate, moe_w_up, moe_w_down,
           shared_w_gate, shared_w_up, shared_w_down):
    bsz, seq, d = x.shape
    ctx_len = ctx.shape[1]
    assert d == D and seq % TM == 0 and ctx_len % TM == 0 and seq % GRID_W == 0
    assert DEPTH % 2 == 0
    nct = ctx_len // TM
    xs = jnp.concatenate([ctx, x], axis=1)
    n_all = bsz * (ctx_len + seq)
    n_slots = (n_all + N_CLASS * (MB - 1) + MB - 1) // MB * MB
    xs_buf = jnp.zeros((n_slots, D + EXT), F32)

    rows = (bsz + 1 + 7) // 8 * 8
    cs = jnp.zeros((rows, D), F32).at[0:bsz].set(c).at[bsz].set(c_ctx)
    mod = _modulation(cs, mod_w, mod_b)

    def table(part):
        lat = part[0:bsz]
        cx = jnp.broadcast_to(part[bsz][None, :], (bsz, D))
        return jnp.stack([cx, lat], axis=1)[:, :, None, :]

    consts = _rwkv_consts()
    rope = _rope_tables(ctx_len, seq)
    cs_mat, dlat, dctx = _fourier_consts(ctx_len, seq)
    rt = _route_consts(router_w, router_b)
    nctx_chunks = ctx_len // CH

    for i in range(DEPTH):
        j = i // 2
        t0 = nct if i == DEPTH - 1 else 0
        sh1, sc1, g1, sh2, sc2, g2 = [mod[i][:, D * q:D * (q + 1)] for q in range(6)]
        t_sc1 = table((1.0 + sc1) * norm1_g[i][None, :])
        t_sh1 = table(sh1)
        t_g1 = table(g1)
        t_sc2 = table((1.0 + sc2) * norm2_g[i][None, :])
        t_sh2 = table(sh2)
        t_g2 = table(g2)
        if i % 2 == 0:
            w, rw, mla, wo = _even_params(
                j, w_in, a_mu, a_w0, a_w2, a_a0, a_a2, a_g2, a_k_k, a_k_a, a_r_k, a_ln_w, a_ln_b,
                b_q_norm, b_w_q_up, b_kv_norm, b_w_kv_up, b_q_gain, b_k_gain, w_out_even)
            pa, pb = _in_proj(xs, t_sc1, t_sh1, w, nct)
            yf = _rwkv_pass(pa, rw, consts, None, reverse=False, nctx=nctx_chunks)
            ya = _rwkv_pass(pa, rw, consts, yf, reverse=True, nctx=nctx_chunks)
            q, k, v = _mla_prep(pb, mla, rope)
            yb = _attention(q, k, v, ctx_len)
            xs, he, cnt = _even_out(ya, yb, wo, xs, t_g1, t_sc2, t_sh2, rt, nct)
        else:
            hc, hs = _fourier_channels(xs, t_sc1, t_sh1, cs_mat, nct)
            xs, he, cnt = _fourier_out(hc, hs, dlat, dctx, w_out_fourier[j].astype(BF16), xs,
                                       t_g1, t_sc2, t_sh2, rt, ctx_len, t0)
        xs, xs_buf = _moe_layer(
            xs, he, cnt, xs_buf, t_g2,
            moe_w_gate[i].astype(BF16), moe_w_up[i].astype(BF16), moe_w_down[i].astype(BF16),
            shared_w_gate[i].astype(BF16), shared_w_up[i].astype(BF16),
            shared_w_down[i].astype(BF16), nct, t0)
    return xs
```

```python
import functools
import math

import numpy as np
import jax
import jax.numpy as jnp
from jax import lax
from jax.experimental import pallas as pl
from jax.experimental.pallas import tpu as pltpu

F32 = jnp.float32
BF16 = jnp.bfloat16
HI = lax.Precision.HIGHEST

D = 1024
DEPTH = 4
GRID_W = 64
NORM_EPS = 1e-6
A_HEAD = 64
A_HEADS = 8
A_W = 512
A_RANK = 64
A_GATE = 128
A_DECAY_SCALE = 0.606531
A_LN_EPS = 64e-5
A_COLS = 3 * A_W + 4 * A_RANK + A_GATE
B_HEADS = 8
B_NOPE = 64
B_ROPE = 32
B_QK = 96
B_V = 64
B_Q_RANK = 256
B_KV_RANK = 128
B_COLS = B_Q_RANK + B_KV_RANK + B_ROPE
ATTN_SCALE = B_QK ** -0.5
LOG2E = math.log2(math.e)
ROPE_BASE = 10000.0
C_GROUPS = 8
C_GDIM = 128
N_EXPERTS = 16
N_GROUPS = 4
EPG = 4
D_EXPERT = 512
D_SHARED = 512

LANE = 128
TM = 256
CH = 64
RW_NB = 4
ATT_NQ = 4
MB = 256
PERM_ROWS = 1024
PB_COLS = 512
HP = 128
N_CLASS = 24
EXT = 128
PAIRS = ((0, 1), (0, 2), (0, 3), (1, 2), (1, 3), (2, 3))
VMEM_LIMIT = 56 * 1024 * 1024


def _cparams(sem):
    return pltpu.CompilerParams(dimension_semantics=sem, vmem_limit_bytes=VMEM_LIMIT)


def _mm(a, b):
    return jnp.dot(a.astype(BF16), b.astype(BF16), preferred_element_type=F32)


def _mm_nt(a, b):
    return lax.dot_general(a.astype(BF16), b.astype(BF16), (((1,), (1,)), ((), ())),
                           preferred_element_type=F32)


def _mm_tn(a, b):
    return lax.dot_general(a.astype(BF16), b.astype(BF16), (((0,), (0,)), ((), ())),
                           preferred_element_type=F32)


def _mm_hi(a, b):
    return jnp.dot(a, b, precision=HI, preferred_element_type=F32)


def _split_bf16(x):
    hi = x.astype(BF16)
    return hi, (x - hi.astype(F32)).astype(BF16)


def _mm_split_rhs(a, b):
    hi, lo = _split_bf16(b)
    a = a.astype(BF16)
    return jnp.dot(a, hi, preferred_element_type=F32) + jnp.dot(a, lo, preferred_element_type=F32)


def _mm_split_lhs(a, b):
    hi, lo = _split_bf16(a)
    m = a.shape[0]
    r = jnp.dot(jnp.concatenate([hi, lo], axis=0), b.astype(BF16), preferred_element_type=F32)
    return r[0:m] + r[m:2 * m]


def _sigmoid(x):
    return 1.0 / (1.0 + jnp.exp(-x))


def _silu(x):
    return x * _sigmoid(x)


def _rms_mod(x, scale, shift):
    ms = jnp.mean(x * x, axis=-1, keepdims=True)
    return x * lax.rsqrt(ms + NORM_EPS) * scale + shift


def _mod_kernel(c_ref, w_ref, b_ref, o_ref):
    o_ref[...] = _mm_hi(_silu(c_ref[...]), w_ref[...]) + b_ref[...]


def _modulation(cs, mod_w, mod_b):
    rows = cs.shape[0]
    tn = 1536
    return pl.pallas_call(
        _mod_kernel,
        grid=(DEPTH, 6 * D // tn),
        in_specs=[pl.BlockSpec((rows, D), lambda l, n: (0, 0)),
                  pl.BlockSpec((None, D, tn), lambda l, n: (l, 0, n)),
                  pl.BlockSpec((None, 1, tn), lambda l, n: (l, 0, n))],
        out_specs=pl.BlockSpec((None, rows, tn), lambda l, n: (l, 0, n)),
        out_shape=jax.ShapeDtypeStruct((DEPTH, rows, 6 * D), F32),
        compiler_params=_cparams(("arbitrary", "arbitrary")),
        name="modulation",
    )(cs, mod_w, mod_b.reshape(DEPTH, 1, 6 * D))


def _tab_spec(nct, t0=0):
    return pl.BlockSpec((None, None, 1, D),
                        lambda b, t, *_: (b, jnp.minimum((t + t0) // nct, 1), 0, 0))


def _e1_kernel(x_ref, sc_ref, sh_ref, w_ref, pa_ref, pb_ref):
    h = _rms_mod(x_ref[...], sc_ref[...], sh_ref[...])
    p = _mm(h, w_ref[...])
    pa_ref[...] = p[:, :A_COLS]
    pb_ref[...] = p[:, A_COLS:]


def _in_proj(x, sc, sh, w, nct):
    bsz, s, _ = x.shape
    ncol = A_COLS + PB_COLS
    return pl.pallas_call(
        _e1_kernel,
        grid=(bsz, s // TM),
        in_specs=[pl.BlockSpec((None, TM, D), lambda b, t: (b, t, 0)),
                  _tab_spec(nct), _tab_spec(nct),
                  pl.BlockSpec((D, ncol), lambda b, t: (0, 0))],
        out_specs=[pl.BlockSpec((None, TM, A_COLS), lambda b, t: (b, t, 0)),
                   pl.BlockSpec((None, TM, PB_COLS), lambda b, t: (b, t, 0))],
        out_shape=[jax.ShapeDtypeStruct((bsz, s, A_COLS), F32),
                   jax.ShapeDtypeStruct((bsz, s, PB_COLS), F32)],
        compiler_params=_cparams(("parallel", "parallel")),
        name="in_proj",
    )(x, sc, sh, w)


def _chunk_of(i, nctx, nch, reverse):
    if not reverse:
        return i
    return jnp.where(i < nctx, nctx - 1 - i, nch - 1 - (i - nctx))


def _rw_kernel(*refs, reverse, final, nctx, nch, nb):
    if final:
        (pc_ref, pp_ref, pn_ref, mu_ref, w0_ref, w2_ref, a0_ref, a2_ref, kk_ref, ka_ref,
         tri_ref, ms_ref, mi_ref, bones_ref,
         a0o_ref, a2o_ref, g2_ref, rk_ref, lnw_ref, lnb_ref, yf_ref, out_ref, z_ref) = refs
    else:
        (pc_ref, pp_ref, pn_ref, mu_ref, w0_ref, w2_ref, a0_ref, a2_ref, kk_ref, ka_ref,
         tri_ref, ms_ref, mi_ref, bones_ref, out_ref, z_ref) = refs
    i = pl.program_id(1)
    c = _chunk_of(i, nctx, nch, reverse)

    @pl.when(i == 0)
    def _():
        z_ref[...] = jnp.zeros_like(z_ref)

    seg_first = jnp.logical_or(c == 0, c == nctx)
    seg_last = jnp.logical_or(c == nctx - 1, c == nch - 1)
    rid = lax.broadcasted_iota(jnp.int32, (CH, 1), 0)
    bones = bones_ref[...]
    lane = lax.broadcasted_iota(jnp.int32, (1, LANE), 1)
    m0 = (lane < A_HEAD).astype(F32)
    m1 = 1.0 - m0
    eye = (lax.broadcasted_iota(jnp.int32, (LANE, LANE), 0)
           == lax.broadcasted_iota(jnp.int32, (LANE, LANE), 1))
    eye_c = rid == (lane & (A_HEAD - 1))
    mstrict = ms_ref[...] > 0.5
    mincl = mi_ref[...] > 0.5
    h0 = lane < A_HEAD
    h0w = jnp.concatenate([h0, h0], axis=1)
    zero_b = jnp.zeros((), BF16)

    def stack(x):
        x = x.astype(BF16)
        sel = h0 if x.shape[1] == LANE else h0w
        return jnp.concatenate([jnp.where(sel, x, zero_b), jnp.where(sel, zero_b, x)], axis=0)

    pre = []
    chains = []
    for n in range(nb):
        p = pc_ref[n]
        prow = jnp.where(seg_first, 0.0, pp_ref[n, 7:8, :])
        nrow = jnp.where(seg_last, 0.0, pn_ref[n, 0:1, :])
        pprev = jnp.where(rid == 0, prow, pltpu.roll(p, 1, 0))
        pnext = jnp.where(rid == CH - 1, nrow, pltpu.roll(p, CH - 1, 0))
        ps = p + mu_ref[...] * (0.5 * (pprev + pnext) - p)
        r = ps[:, 0:A_W]
        k = ps[:, A_W:2 * A_W]
        v = ps[:, 2 * A_W:3 * A_W]
        dd = ps[:, 3 * A_W:3 * A_W + 128]
        ad = ps[:, 3 * A_W + 128:3 * A_W + 256]
        gd = ps[:, 3 * A_W + 256:3 * A_W + 384]
        logw = -A_DECAY_SCALE * _sigmoid(w0_ref[...] + _mm(jnp.tanh(dd), w2_ref[...]))
        aic = _sigmoid(a0_ref[...] + _mm(ad, a2_ref[...]))
        kkr = k * kk_ref[...]
        kk = kkr * lax.rsqrt(_mm_split_lhs(kkr * kkr, bones) + 1e-12)
        kd = k * (1.0 + (aic - 1.0) * ka_ref[...])
        bd = kk * aic
        cum = _mm_split_rhs(tri_ref[...], logw)
        tot = cum[0:1, :] if reverse else cum[CH - 1:CH, :]
        rt = r * jnp.exp(cum)
        at = -kk * jnp.exp(cum - logw)
        ei = jnp.exp(-cum)
        kt = kd * ei
        bt = bd * ei
        et = jnp.exp(tot - cum)
        kh = kd * et
        bh = bd * et
        gam = jnp.exp(tot)
        pre.append((r, k, v, ad, gd, kd))
        for j in range(A_HEADS // 2):
            sl = slice(LANE * j, LANE * (j + 1))
            chains.append({
                "n": n, "j": j, "gam": gam[:, sl], "a": at[:, sl], "r": rt[:, sl],
                "bts": stack(bt[:, sl]), "kts": stack(kt[:, sl]), "vs": stack(v[:, sl]),
                "bhs": stack(bh[:, sl]), "khs": stack(kh[:, sl])})

    for ch in chains:
        big = _mm_nt(jnp.concatenate([ch["a"], ch["r"]], axis=0),
                     jnp.concatenate([ch["bts"], ch["kts"]], axis=0))
        ch["nm"] = jnp.where(mstrict, big[0:CH, 0:128], 0.0)
        aak = jnp.where(mstrict, big[0:CH, 128:256], 0.0)
        ch["arb"] = jnp.where(mincl, big[CH:2 * CH, 0:128], 0.0)
        ark = jnp.where(mincl, big[CH:2 * CH, 128:256], 0.0)
        ch["aa"] = jnp.concatenate([aak, ark], axis=0)
    for ch in chains:
        avs = _mm(ch["aa"], ch["vs"])
        ch["rhs"] = jnp.concatenate([ch["a"], avs[0:CH]], axis=1)
        ch["av2"] = avs[CH:2 * CH]
        ch["t"] = jnp.where(eye_c, 1.0, ch["nm"])
    for it in range(5):
        for ch in chains:
            n2 = _mm(ch["nm"], stack(ch["nm"]))
            ch["t"] = ch["t"] + _mm(n2, stack(ch["t"]))
            ch["nm"] = n2
    for ch in chains:
        ch["x"] = stack(_mm(ch["t"], stack(ch["rhs"])))
    ys = [[None] * (A_HEADS // 2) for _ in range(nb)]
    for ch in chains:
        x = ch["x"]
        qy = jnp.concatenate([ch["r"], ch["av2"]], axis=1) + _mm(ch["arb"], x)
        z = z_ref[ch["n"], ch["j"]]
        ys[ch["n"]][ch["j"]] = qy[:, 128:256] + _mm(qy[:, 0:128], z)
        lhs = jnp.concatenate([ch["bhs"], ch["khs"]], axis=0)
        rhs = jnp.concatenate(
            [x, jnp.concatenate([jnp.zeros((128, 128), BF16), ch["vs"]], axis=1)], axis=0)
        m = _mm_tn(lhs, rhs)
        gt = jnp.where(eye, ch["gam"], 0.0) + m[:, 0:128]
        z_ref[ch["n"], ch["j"]] = _mm(gt, z) + m[:, 128:256]

    for n in range(nb):
        y = jnp.concatenate(ys[n], axis=1)
        if not final:
            out_ref[n] = y
        else:
            r, k, v, ad, gd, kd = pre[n]
            y = y + yf_ref[n]
            aico = _sigmoid(a0o_ref[...] + _mm(ad, a2o_ref[...]))
            kdo = k * (1.0 + (aico - 1.0) * ka_ref[...])
            sums = _mm_split_lhs(jnp.concatenate([y, r * rk_ref[...] * (kd + kdo)], axis=0), bones)
            yc = y - sums[0:CH] * (1.0 / A_HEAD)
            var = _mm_split_lhs(yc * yc, bones) * (1.0 / A_HEAD)
            yn = yc * lax.rsqrt(var + A_LN_EPS) * lnw_ref[...] + lnb_ref[...]
            bonus = sums[CH:2 * CH] * v
            g = _mm(_sigmoid(gd), g2_ref[...])
            out_ref[n] = ((yn + bonus) * g).astype(out_ref.dtype)


def _rwkv_pass(pa, prm, consts, yf, *, reverse, nctx):
    bsz, s, _ = pa.shape
    nch = s // CH
    final = yf is not None
    d = 1 if reverse else 0
    nb = RW_NB
    assert bsz % nb == 0

    def cmap(b, i):
        return (b, _chunk_of(i, nctx, nch, reverse), 0)

    def pmap(b, i):
        return (b, jnp.maximum(_chunk_of(i, nctx, nch, reverse) * (CH // 8) - 1, 0), 0)

    def nmap(b, i):
        return (b, jnp.minimum((_chunk_of(i, nctx, nch, reverse) + 1) * (CH // 8), s // 8 - 1), 0)

    def full(a):
        return pl.BlockSpec(a.shape, lambda b, i: (0,) * a.ndim)

    args = [pa, pa, pa, prm["mu"], prm["w0"][d], prm["w2"][d], prm["a0"][d], prm["a2"][d],
            prm["kk"], prm["ka"], consts["tri"][d], consts["mstrict"][d], consts["mincl"][d],
            consts["bones"]]
    specs = [pl.BlockSpec((nb, CH, A_COLS), cmap),
             pl.BlockSpec((nb, 8, A_COLS), pmap),
             pl.BlockSpec((nb, 8, A_COLS), nmap)] + [full(a) for a in args[3:]]
    if final:
        extra = [prm["a0"][1 - d], prm["a2"][1 - d], prm["g2"], prm["rk"], prm["lnw"], prm["lnb"]]
        args += extra + [yf]
        specs += [full(a) for a in extra] + [pl.BlockSpec((nb, CH, A_W), cmap)]
    return pl.pallas_call(
        functools.partial(_rw_kernel, reverse=reverse, final=final, nctx=nctx, nch=nch, nb=nb),
        grid=(bsz // nb, nch),
        in_specs=specs,
        out_specs=pl.BlockSpec((nb, CH, A_W), cmap),
        out_shape=jax.ShapeDtypeStruct((bsz, s, A_W), BF16 if final else F32),
        scratch_shapes=[pltpu.VMEM((nb, A_HEADS // 2, LANE, LANE), F32)],
        compiler_params=_cparams(("parallel", "arbitrary")),
        name="rwkv_bwd" if reverse else "rwkv_fwd",
    )(*args)


def _mla_prep_kernel(pb_ref, qn_ref, kvn_ref, wq_ref, wk_ref, qg_ref, kg_ref, rp_ref,
                     c_ref, s1_ref, q_ref, k_ref, v_ref):
    pb = pb_ref[...]
    pq = pb[:, 0:B_Q_RANK]
    xq = pq * lax.rsqrt(jnp.mean(pq * pq, axis=-1, keepdims=True) + NORM_EPS) * qn_ref[...]
    q = _mm(xq, wq_ref[...])
    pk = pb[:, B_Q_RANK:PB_COLS]
    is_kv = lax.broadcasted_iota(jnp.int32, (1, PB_COLS - B_Q_RANK), 1) < B_KV_RANK
    ms = jnp.sum(jnp.where(is_kv, pk * pk, 0.0), axis=-1, keepdims=True) * (1.0 / B_KV_RANK)
    xk = jnp.where(is_kv, pk * lax.rsqrt(ms + NORM_EPS) * kvn_ref[...], pk)
    kv = _mm(xk, wk_ref[...])
    cc, s1 = c_ref[...], s1_ref[...]

    def head(xh, gain):
        ss = jnp.sum(xh * xh, axis=-1, keepdims=True) * (1.0 / B_QK)
        xh = xh * lax.rsqrt(ss + NORM_EPS) * gain
        return xh * cc + _mm(xh, rp_ref[...]) * s1

    for h in range(B_HEADS):
        sl = slice(HP * h, HP * (h + 1))
        q_ref[:, sl] = (head(q[:, sl], qg_ref[...]) * (ATTN_SCALE * LOG2E)).astype(BF16)
        k_ref[:, sl] = head(kv[:, sl], kg_ref[...]).astype(BF16)
    v_ref[...] = kv[:, B_HEADS * HP:].astype(BF16)


def _mla_prep(pb, prm, rope):
    bsz, s, _ = pb.shape

    def full(a):
        return pl.BlockSpec(a.shape, lambda b, t: (0,) * a.ndim)

    consts = [prm["qn"], prm["kvn"], prm["wq"], prm["wk"], prm["qg"], prm["kg"], rope[0]]
    rspec = pl.BlockSpec((TM, HP), lambda b, t: (t, 0))
    return pl.pallas_call(
        _mla_prep_kernel,
        grid=(bsz, s // TM),
        in_specs=[pl.BlockSpec((None, TM, PB_COLS), lambda b, t: (b, t, 0))]
        + [full(a) for a in consts] + [rspec, rspec],
        out_specs=[pl.BlockSpec((None, TM, B_HEADS * HP), lambda b, t: (b, t, 0)),
                   pl.BlockSpec((None, TM, B_HEADS * HP), lambda b, t: (b, t, 0)),
                   pl.BlockSpec((None, TM, B_HEADS * B_V), lambda b, t: (b, t, 0))],
        out_shape=[jax.ShapeDtypeStruct((bsz, s, B_HEADS * HP), BF16),
                   jax.ShapeDtypeStruct((bsz, s, B_HEADS * HP), BF16),
                   jax.ShapeDtypeStruct((bsz, s, B_HEADS * B_V), BF16)],
        compiler_params=_cparams(("parallel", "parallel")),
        name="mla_prep",
    )(pb, *consts, *rope[1:])


def _att_kernel(*refs, nq):
    q_refs, (k_ref, v_ref, o_ref) = refs[:nq], refs[nq:]
    lane = lax.broadcasted_iota(jnp.int32, (1, 2 * B_V), 1)
    outs = []
    for s in range(2):
        q = jnp.concatenate([r[:, HP * s:HP * (s + 1)] for r in q_refs], axis=0)
        k = k_ref[:, HP * s:HP * (s + 1)]
        sc = lax.dot_general(q, k, (((1,), (1,)), ((), ())), preferred_element_type=F32)
        p = jnp.exp2(sc - jnp.max(sc, axis=-1, keepdims=True)).astype(BF16)
        ones_lane = B_V * (1 - s)
        vm = jnp.where((lane >= B_V * s) & (lane < B_V * (s + 1)), v_ref[...],
                       jnp.where(lane == ones_lane, 1.0, 0.0).astype(BF16))
        acc = jnp.dot(p, vm, preferred_element_type=F32)
        outs.append(acc / acc[:, ones_lane:ones_lane + 1])
    o_ref[...] = jnp.where(lane < B_V, outs[0], outs[1]).astype(BF16)


def _attention(q, k, v, ctx_len):
    bsz, s, _ = q.shape
    nct = ctx_len // TM
    nlat = (s - ctx_len) // TM
    nq = ATT_NQ if nlat % ATT_NQ == 0 else 1

    def call(nq, nk, tiles, t0, name):
        q_specs = [pl.BlockSpec((None, TM, 2 * HP), lambda b, j, t, i=i: (b, t0 + nq * t + i, j))
                   for i in range(nq)]
        return pl.pallas_call(
            functools.partial(_att_kernel, nq=nq),
            grid=(bsz, B_HEADS // 2, tiles // nq),
            in_specs=q_specs + [pl.BlockSpec((None, nk, 2 * HP), lambda b, j, t: (b, 0, j)),
                                pl.BlockSpec((None, nk, 2 * B_V), lambda b, j, t: (b, 0, j))],
            out_specs=pl.BlockSpec((None, nq * TM, 2 * B_V), lambda b, j, t: (b, t, j)),
            out_shape=jax.ShapeDtypeStruct((bsz, tiles * TM, B_HEADS * B_V), BF16),
            compiler_params=_cparams(("parallel", "parallel", "parallel")),
            name=name,
        )(*([q] * nq), k, v)

    return call(1, ctx_len, nct, 0, "mla_attention_ctx"), call(nq, s, nlat, nct, "mla_attention")


def _route_tail(x, o, g1, sc2, sh2, rwh_ref, rwl_ref, rb_ref, pm_ref, pab_ref, stril_ref,
                xo_ref, he_ref, cnt_ref, cnt_scr, first):
    xn = x + g1 * o
    xo_ref[...] = xn
    h2 = _rms_mod(xn, sc2, sh2)
    he_ref[:, 0:D] = h2

    @pl.when(first)
    def _():
        cnt_scr[...] = jnp.zeros_like(cnt_scr)

    lane = lax.broadcasted_iota(jnp.int32, (1, LANE), 1)
    h_hi, h_lo = _split_bf16(h2)
    r_hi, r_lo = rwh_ref[...], rwl_ref[...]
    logits = (jnp.dot(h_hi, r_hi, preferred_element_type=F32)
              + jnp.dot(h_lo, r_hi, preferred_element_type=F32)
              + jnp.dot(h_hi, r_lo, preferred_element_type=F32))
    scores = _sigmoid(logits)
    sel = scores + rb_ref[...]
    ps = jnp.where(lane < N_CLASS, _mm_split_lhs(sel, pm_ref[...]), -jnp.inf)
    best = jnp.max(ps, axis=-1, keepdims=True)
    cls = jnp.min(jnp.where(ps == best, lane, LANE), axis=-1, keepdims=True)
    onehot = (lane == cls).astype(F32)
    s_hi, s_lo = _split_bf16(scores)
    sab = (jnp.dot(s_hi, pab_ref[...], preferred_element_type=F32)
           + jnp.dot(s_lo, pab_ref[...], preferred_element_type=F32))
    sa = jnp.sum(onehot * sab[:, 0:LANE], axis=-1, keepdims=True)
    sb = jnp.sum(onehot * sab[:, LANE:2 * LANE], axis=-1, keepdims=True)
    den = sa + sb
    prefix = jnp.dot(stril_ref[...], onehot.astype(BF16), preferred_element_type=F32)
    rank = jnp.sum(onehot * (prefix + cnt_scr[0:1, :]), axis=-1, keepdims=True)
    cnt_scr[...] = cnt_scr[...] + jnp.sum(onehot, axis=0, keepdims=True)
    cnt_ref[...] = cnt_scr[...]
    he_ref[:, D:D + EXT] = jnp.where(
        lane == 0, sa / den,
        jnp.where(lane == 1, sb / den,
                  jnp.where(lane == 2, cls.astype(F32), jnp.where(lane == 3, rank, 0.0))))


def _tail_out(bsz, s_out):
    specs = [pl.BlockSpec((None, TM, D), lambda b, t: (b, t, 0)),
             pl.BlockSpec((None, TM, D + EXT), lambda b, t: (b, t, 0)),
             pl.BlockSpec((8, LANE), lambda b, t: (0, 0))]
    shapes = [jax.ShapeDtypeStruct((bsz, s_out, D), F32),
              jax.ShapeDtypeStruct((bsz, s_out, D + EXT), F32),
              jax.ShapeDtypeStruct((8, LANE), F32)]
    return specs, shapes


def _e2_kernel(ya_ref, ybc_ref, ybl_ref, wo_ref, x_ref, g1_ref, sc_ref, sh_ref, rwh_ref, rwl_ref, rb_ref,
               pm_ref, pab_ref, stril_ref, xo_ref, he_ref, cnt_ref, cnt_scr, *, nct):
    yb = jnp.where(pl.program_id(1) < nct, ybc_ref[...], ybl_ref[...])
    o = (jnp.dot(ya_ref[...], wo_ref[0:A_W, :], preferred_element_type=F32)
         + jnp.dot(yb, wo_ref[A_W:2 * A_W, :], preferred_element_type=F32))
    first = jnp.logical_and(pl.program_id(0) == 0, pl.program_id(1) == 0)
    _route_tail(x_ref[...], o, g1_ref[...], sc_ref[...], sh_ref[...], rwh_ref, rwl_ref, rb_ref, pm_ref,
                pab_ref, stril_ref, xo_ref, he_ref, cnt_ref, cnt_scr, first)


def _even_out(ya, yb, wo, x, g1, sc2, sh2, rt, nct):
    bsz, s, _ = x.shape
    out_specs, out_shape = _tail_out(bsz, s)
    yb_ctx, yb_lat = yb

    def full(a):
        return pl.BlockSpec(a.shape, lambda b, t: (0,) * a.ndim)

    return pl.pallas_call(
        functools.partial(_e2_kernel, nct=nct),
        grid=(bsz, s // TM),
        in_specs=[pl.BlockSpec((None, TM, A_W), lambda b, t: (b, t, 0)),
                  pl.BlockSpec((None, TM, A_W), lambda b, t: (b, jnp.minimum(t, nct - 1), 0)),
                  pl.BlockSpec((None, TM, A_W), lambda b, t: (b, jnp.maximum(t - nct, 0), 0)),
                  full(wo),
                  pl.BlockSpec((None, TM, D), lambda b, t: (b, t, 0)),
                  _tab_spec(nct), _tab_spec(nct), _tab_spec(nct)] + [full(a) for a in rt],
        out_specs=out_specs,
        out_shape=out_shape,
        scratch_shapes=[pltpu.VMEM((8, LANE), F32)],
        compiler_params=_cparams(("arbitrary", "arbitrary")),
        name="even_out_route",
    )(ya, yb_ctx, yb_lat, wo, x, g1, sc2, sh2, *rt)


def _f1_kernel(x_ref, sc_ref, sh_ref, cs_ref, hc_ref, hs_ref):
    h = _rms_mod(x_ref[...], sc_ref[...], sh_ref[...]).astype(BF16)
    for g in range(C_GROUPS):
        sl = slice(C_GDIM * g, C_GDIM * (g + 1))
        r = jnp.dot(h[:, sl], cs_ref[...], preferred_element_type=F32)
        hc_ref[:, sl] = r[:, 0:C_GDIM].astype(BF16)
        hs_ref[:, sl] = r[:, C_GDIM:2 * C_GDIM].astype(BF16)


def _fourier_channels(x, sc, sh, cs, nct):
    bsz, s, _ = x.shape
    return pl.pallas_call(
        _f1_kernel,
        grid=(bsz, s // TM),
        in_specs=[pl.BlockSpec((None, TM, D), lambda b, t: (b, t, 0)),
                  _tab_spec(nct), _tab_spec(nct),
                  pl.BlockSpec(cs.shape, lambda b, t: (0, 0))],
        out_specs=[pl.BlockSpec((None, TM, D), lambda b, t: (b, t, 0)),
                   pl.BlockSpec((None, TM, D), lambda b, t: (b, t, 0))],
        out_shape=[jax.ShapeDtypeStruct((bsz, s, D), BF16),
                   jax.ShapeDtypeStruct((bsz, s, D), BF16)],
        compiler_params=_cparams(("parallel", "parallel")),
        name="fourier_channels",
    )(x, sc, sh, cs)


def _f2_kernel(hc_ref, hs_ref, dl_ref, dc_ref, wo_ref, x_ref, g1_ref, sc_ref, sh_ref, rwh_ref, rwl_ref,
               rb_ref, pm_ref, pab_ref, stril_ref, xo_ref, he_ref, cnt_ref, cnt_scr, f_scr,
               *, ctx_len, nct, t0):
    t = pl.program_id(1) + t0
    s = hc_ref.shape[0]
    seq = s - ctx_len

    if t0 < nct:
        @pl.when(t < nct)
        def _():
            f_scr[...] = (jnp.dot(dc_ref[:, 0:ctx_len], hc_ref[0:ctx_len, :], preferred_element_type=F32)
                          + jnp.dot(dc_ref[:, ctx_len:2 * ctx_len], hs_ref[0:ctx_len, :],
                                    preferred_element_type=F32))

    @pl.when(t >= nct)
    def _():
        f_scr[...] = (jnp.dot(dl_ref[:, 0:seq], hc_ref[ctx_len:s, :], preferred_element_type=F32)
                      + jnp.dot(dl_ref[:, seq:2 * seq], hs_ref[ctx_len:s, :],
                                preferred_element_type=F32))

    o = _mm(f_scr[...], wo_ref[...])
    first = jnp.logical_and(pl.program_id(0) == 0, pl.program_id(1) == 0)
    _route_tail(x_ref[...], o, g1_ref[...], sc_ref[...], sh_ref[...], rwh_ref, rwl_ref, rb_ref, pm_ref,
                pab_ref, stril_ref, xo_ref, he_ref, cnt_ref, cnt_scr, first)


def _fourier_out(hc, hs, dlat, dctx, wo, x, g1, sc2, sh2, rt, ctx_len, t0):
    bsz, s, _ = x.shape
    nct = ctx_len // TM
    nt = s // TM - t0
    out_specs, out_shape = _tail_out(bsz, nt * TM)

    def full(a):
        return pl.BlockSpec(a.shape, lambda b, t: (0,) * a.ndim)

    return pl.pallas_call(
        functools.partial(_f2_kernel, ctx_len=ctx_len, nct=nct, t0=t0),
        grid=(bsz, nt),
        in_specs=[pl.BlockSpec((None, s, D), lambda b, t: (b, 0, 0)),
                  pl.BlockSpec((None, s, D), lambda b, t: (b, 0, 0)),
                  pl.BlockSpec((TM, dlat.shape[1]), lambda b, t: (jnp.maximum(t + t0 - nct, 0), 0)),
                  pl.BlockSpec((TM, dctx.shape[1]), lambda b, t: (jnp.minimum(t + t0, nct - 1), 0)),
                  full(wo),
                  pl.BlockSpec((None, TM, D), lambda b, t: (b, t + t0, 0)),
                  _tab_spec(nct, t0), _tab_spec(nct, t0), _tab_spec(nct, t0)]
        + [full(a) for a in rt],
        out_specs=out_specs,
        out_shape=out_shape,
        scratch_shapes=[pltpu.VMEM((8, LANE), F32), pltpu.VMEM((TM, D), F32)],
        compiler_params=_cparams(("arbitrary", "arbitrary")),
        name="fourier_out_route",
    )(hc, hs, dlat, dctx, wo, x, g1, sc2, sh2, *rt)


def _row_copy(src_ref, src_row, dst_ref, dst_row, sem):
    return pltpu.make_async_copy(src_ref.at[pl.ds(src_row, 1)], dst_ref.at[pl.ds(dst_row, 1)], sem)


def _scatter_kernel(slot_ref, h_ref, xin_ref, xs_ref, sem, *, rows):
    del xin_ref
    base = pl.program_id(0) * rows

    def issue(r, carry):
        _row_copy(h_ref, r, xs_ref, slot_ref[base + r], sem).start()
        return carry

    lax.fori_loop(0, rows, issue, 0, unroll=8)
    pltpu.make_async_copy(h_ref, xs_ref.at[pl.ds(0, rows)], sem).wait()


def _scatter_rows(slot, src, dst):
    n = src.shape[0]
    rows = math.gcd(n, PERM_ROWS)
    return pl.pallas_call(
        functools.partial(_scatter_kernel, rows=rows),
        grid_spec=pltpu.PrefetchScalarGridSpec(
            num_scalar_prefetch=1,
            grid=(n // rows,),
            in_specs=[pl.BlockSpec((rows, src.shape[1]), lambda i, sl: (i, 0)),
                      pl.BlockSpec(memory_space=pl.ANY)],
            out_specs=pl.BlockSpec(memory_space=pl.ANY),
            scratch_shapes=[pltpu.SemaphoreType.DMA(())]),
        out_shape=jax.ShapeDtypeStruct(dst.shape, dst.dtype),
        input_output_aliases={2: 0},
        compiler_params=_cparams(("arbitrary",)),
        name="moe_scatter_rows",
    )(slot, src, dst)


def _moe_kernel(ea_ref, eb_ref, ok_ref, xs_ref, wga_ref, wgb_ref, wua_ref, wub_ref, wda_ref, wdb_ref,
                y_ref):
    i = pl.program_id(0)

    @pl.when(ok_ref[i] > 0)
    def _():
        xe = xs_ref[...]
        x = xe[:, 0:D].astype(BF16)
        wa = xe[:, D:D + 1]
        wb = xe[:, D + 1:D + 2]
        ha = _silu(jnp.dot(x, wga_ref[...], preferred_element_type=F32)) \
            * jnp.dot(x, wua_ref[...], preferred_element_type=F32) * wa
        hb = _silu(jnp.dot(x, wgb_ref[...], preferred_element_type=F32)) \
            * jnp.dot(x, wub_ref[...], preferred_element_type=F32) * wb
        y_ref[...] = (jnp.dot(ha.astype(BF16), wda_ref[...], preferred_element_type=F32)
                      + jnp.dot(hb.astype(BF16), wdb_ref[...], preferred_element_type=F32))

    @pl.when(ok_ref[i] == 0)
    def _():
        y_ref[...] = jnp.zeros_like(y_ref)


def _moe_blocks(ea, eb, ok, xs, wg, wu, wd):
    n_slots = xs.shape[0]
    wa_spec = pl.BlockSpec((None, D, D_EXPERT), lambda i, ea, eb, ok: (ea[i], 0, 0))
    wb_spec = pl.BlockSpec((None, D, D_EXPERT), lambda i, ea, eb, ok: (eb[i], 0, 0))
    da_spec = pl.BlockSpec((None, D_EXPERT, D), lambda i, ea, eb, ok: (ea[i], 0, 0))
    db_spec = pl.BlockSpec((None, D_EXPERT, D), lambda i, ea, eb, ok: (eb[i], 0, 0))
    return pl.pallas_call(
        _moe_kernel,
        grid_spec=pltpu.PrefetchScalarGridSpec(
            num_scalar_prefetch=3,
            grid=(n_slots // MB,),
            in_specs=[pl.BlockSpec((MB, D + EXT), lambda i, ea, eb, ok: (i, 0)),
                      wa_spec, wb_spec, wa_spec, wb_spec, da_spec, db_spec],
            out_specs=pl.BlockSpec((MB, D), lambda i, ea, eb, ok: (i, 0))),
        out_shape=jax.ShapeDtypeStruct((n_slots, D), F32),
        compiler_params=_cparams(("arbitrary",)),
        name="moe_experts",
    )(ea, eb, ok, xs, wg, wg, wu, wu, wd, wd)


def _comb_kernel(slot_ref, x_ref, he_ref, g2_ref, wg_ref, wu_ref, wd_ref, y_ref, o_ref, ybuf, sem,
                 *, nt, nsteps):
    step = pl.program_id(0) * nt + pl.program_id(1)
    cur = step % 2

    def gather(tile, buf):
        base = tile * TM

        for r in range(TM):
            _row_copy(y_ref, slot_ref[base + r], ybuf.at[buf], r, sem.at[buf]).start()

    @pl.when(step == 0)
    def _():
        gather(0, 0)

    gather(jnp.minimum(step + 1, nsteps - 1), 1 - cur)
    h = he_ref[:, 0:D].astype(BF16)
    hid = _silu(jnp.dot(h, wg_ref[...], preferred_element_type=F32)) \
        * jnp.dot(h, wu_ref[...], preferred_element_type=F32)
    shared = jnp.dot(hid.astype(BF16), wd_ref[...], preferred_element_type=F32)
    pltpu.make_async_copy(y_ref.at[pl.ds(0, TM)], ybuf.at[cur], sem.at[cur]).wait()
    o_ref[...] = x_ref[...] + g2_ref[...] * (shared + ybuf[cur])

    @pl.when(step == nsteps - 1)
    def _():
        pltpu.make_async_copy(y_ref.at[pl.ds(0, TM)], ybuf.at[1 - cur], sem.at[1 - cur]).wait()


def _combine(slot, x, he, y, g2, wg, wu, wd, nct, t0):
    bsz, s, _ = x.shape
    nt = s // TM

    def full(a):
        return pl.BlockSpec(a.shape, lambda b, t, sl: (0,) * a.ndim)

    return pl.pallas_call(
        functools.partial(_comb_kernel, nt=nt, nsteps=bsz * nt),
        grid_spec=pltpu.PrefetchScalarGridSpec(
            num_scalar_prefetch=1,
            grid=(bsz, nt),
            in_specs=[pl.BlockSpec((None, TM, D), lambda b, t, sl: (b, t, 0)),
                      pl.BlockSpec((None, TM, D + EXT), lambda b, t, sl: (b, t, 0)),
                      _tab_spec(nct, t0), full(wg), full(wu), full(wd),
                      pl.BlockSpec(memory_space=pl.ANY)],
            out_specs=pl.BlockSpec((None, TM, D), lambda b, t, sl: (b, t, 0)),
            scratch_shapes=[pltpu.VMEM((2, TM, D), F32), pltpu.SemaphoreType.DMA((2,))]),
        out_shape=jax.ShapeDtypeStruct((bsz, s, D), F32),
        compiler_params=_cparams(("arbitrary", "arbitrary")),
        name="moe_combine",
    )(slot, x, he, g2, wg, wu, wd, y)


def _moe_layer(x, he, cnt, xs_buf, g2, wg, wu, wd, swg, swu, swd, nct, t0):
    bsz, s, _ = x.shape
    n = bsz * s
    n_slots = xs_buf.shape[0]
    cls = he[:, :, D + 2].reshape(n).astype(jnp.int32)
    rank = he[:, :, D + 3].reshape(n).astype(jnp.int32)
    counts = cnt[0, 0:N_CLASS].astype(jnp.int32)
    padded = (counts + MB - 1) // MB * MB
    pend = jnp.cumsum(padded)
    slot = (pend - padded)[cls] + rank
    bstart = jnp.arange(n_slots // MB, dtype=jnp.int32) * MB
    bcls = jnp.minimum(jnp.sum((bstart[:, None] >= pend[None, :]).astype(jnp.int32), axis=1),
                       N_CLASS - 1)
    ok = (bstart < pend[-1]).astype(jnp.int32)
    pa = jnp.asarray([p[0] for p in PAIRS], jnp.int32)
    pb = jnp.asarray([p[1] for p in PAIRS], jnp.int32)
    ea = EPG * (bcls // len(PAIRS)) + pa[bcls % len(PAIRS)]
    eb = EPG * (bcls // len(PAIRS)) + pb[bcls % len(PAIRS)]
    xs_buf = _scatter_rows(slot, he.reshape(n, D + EXT), xs_buf)
    y = _moe_blocks(ea, eb, ok, xs_buf, wg, wu, wd)
    return _combine(slot, x, he, y, g2, swg, swu, swd, nct, t0), xs_buf


def _rwkv_consts():
    t = np.arange(CH)
    low = (t[:, None] >= t[None, :]).astype(np.float32)
    tri = np.stack([low, low.T])
    t2 = np.arange(2 * CH) % CH
    mstrict = np.stack([t[:, None] > t2[None, :], t[:, None] < t2[None, :]]).astype(np.float32)
    mincl = np.stack([t[:, None] >= t2[None, :], t[:, None] <= t2[None, :]]).astype(np.float32)
    bones = np.kron(np.eye(A_HEADS, dtype=np.float32), np.ones((A_HEAD, A_HEAD), np.float32))
    return {"tri": jnp.asarray(tri), "mstrict": jnp.asarray(mstrict), "mincl": jnp.asarray(mincl),
            "bones": jnp.asarray(bones)}


def _route_consts(router_w, router_b):
    pm = np.zeros((LANE, LANE), np.float32)
    pa = np.zeros((LANE, LANE), np.float32)
    pb = np.zeros((LANE, LANE), np.float32)
    for cidx in range(N_CLASS):
        g, (a, b) = cidx // len(PAIRS), PAIRS[cidx % len(PAIRS)]
        pm[EPG * g + a, cidx] = 1.0
        pm[EPG * g + b, cidx] = 1.0
        pa[EPG * g + a, cidx] = 1.0
        pb[EPG * g + b, cidx] = 1.0
    tt = np.arange(TM)
    stril = (tt[:, None] > tt[None, :]).astype(np.float32)
    rw = jnp.pad(router_w, ((0, 0), (0, LANE - N_EXPERTS)))
    rw_hi = rw.astype(BF16)
    rw_lo = (rw - rw_hi.astype(F32)).astype(BF16)
    rb = jnp.pad(router_b, (0, LANE - N_EXPERTS))[None, :]
    pab = np.concatenate([pa, pb], axis=1)
    return [rw_hi, rw_lo, rb, jnp.asarray(pm, BF16), jnp.asarray(pab, BF16), jnp.asarray(stril, BF16)]


def _rope_tables(ctx_len, seq):
    rows = seq // GRID_W
    row = np.repeat(np.arange(rows), GRID_W).astype(np.float32)
    col = np.tile(np.arange(GRID_W), rows).astype(np.float32)
    half = B_ROPE // 2
    inv = (ROPE_BASE ** (-np.arange(0, half, 2, dtype=np.float32) / half)).astype(np.float32)
    ang = np.concatenate([row[:, None] * inv, col[:, None] * inv], axis=-1).astype(np.float32)
    cos = np.concatenate([np.ones((ctx_len, half), np.float32), np.cos(ang)], axis=0)
    sin = np.concatenate([np.zeros((ctx_len, half), np.float32), np.sin(ang)], axis=0)
    s = ctx_len + seq
    cc = np.zeros((s, HP), np.float32)
    ss = np.zeros((s, HP), np.float32)
    cc[:, 0:B_NOPE] = 1.0
    cc[:, B_NOPE:B_NOPE + half] = cos
    cc[:, B_NOPE + half:B_QK] = cos
    ss[:, B_NOPE:B_NOPE + half] = -sin
    ss[:, B_NOPE + half:B_QK] = sin
    partner = np.zeros((HP, HP), np.float32)
    i = np.arange(half)
    partner[B_NOPE + half + i, B_NOPE + i] = 1.0
    partner[B_NOPE + i, B_NOPE + half + i] = 1.0
    return jnp.asarray(partner, BF16), jnp.asarray(cc), jnp.asarray(ss)


def _dft_rows(n):
    idx = np.arange(n, dtype=np.int64)
    ang = 2.0 * np.pi * ((idx[:, None] * idx[None, :]) % n).astype(np.float64) / n
    scale = 1.0 / np.sqrt(n)
    return np.cos(ang) * scale, np.sin(ang) * scale


def _fourier_consts(ctx_len, seq):
    cc, sc = _dft_rows(C_GDIM)
    cs = np.concatenate([cc, sc], axis=1)
    cl, sl = _dft_rows(seq)
    cx, sx = _dft_rows(ctx_len)
    return (jnp.asarray(cs, BF16), jnp.asarray(np.concatenate([cl, -sl], axis=1), BF16),
            jnp.asarray(np.concatenate([cx, -sx], axis=1), BF16))


def _pad_heads(w, width, pad_to):
    kdim = w.shape[0]
    w = w.reshape(kdim, -1, width)
    w = jnp.pad(w, ((0, 0), (0, 0), (0, pad_to - width)))
    return w.reshape(kdim, -1)


def _even_params(j, w_in, a_mu, a_w0, a_w2, a_a0, a_a2, a_g2, a_k_k, a_k_a, a_r_k, a_ln_w, a_ln_b,
                 b_q_norm, b_w_q_up, b_kv_norm, b_w_kv_up, b_q_gain, b_k_gain, w_out_even):
    w = jnp.pad(w_in[j], ((0, 0), (0, PB_COLS - B_COLS))).astype(BF16)

    def lora_pad(m):
        z = jnp.zeros_like(m[0])
        return jnp.stack([jnp.concatenate([m[0], z], axis=0), jnp.concatenate([z, m[1]], axis=0)])

    rw = {"mu": a_mu[j][None, :],
          "w0": a_w0[j][:, None, :], "w2": lora_pad(a_w2[j]),
          "a0": a_a0[j][:, None, :], "a2": lora_pad(a_a2[j]),
          "g2": a_g2[j], "kk": a_k_k[j][None, :], "ka": a_k_a[j][None, :],
          "rk": a_r_k[j].reshape(1, A_W), "lnw": a_ln_w[j][None, :], "lnb": a_ln_b[j][None, :]}
    kvu = b_w_kv_up[j].reshape(B_KV_RANK, B_HEADS, B_NOPE + B_V)
    wk_nope = _pad_heads(kvu[:, :, :B_NOPE].reshape(B_KV_RANK, -1), B_NOPE, HP)
    wv = kvu[:, :, B_NOPE:].reshape(B_KV_RANK, -1)
    sel = np.zeros((B_ROPE, B_HEADS, HP), np.float32)
    for h in range(B_HEADS):
        sel[np.arange(B_ROPE), h, B_NOPE + np.arange(B_ROPE)] = 1.0
    krows = PB_COLS - B_Q_RANK
    wk = jnp.zeros((krows, B_HEADS * HP + B_HEADS * B_V), F32)
    wk = wk.at[0:B_KV_RANK, 0:B_HEADS * HP].set(wk_nope)
    wk = wk.at[B_KV_RANK:B_KV_RANK + B_ROPE, 0:B_HEADS * HP].set(jnp.asarray(sel.reshape(B_ROPE, -1)))
    wk = wk.at[0:B_KV_RANK, B_HEADS * HP:].set(wv)
    mla = {"qn": b_q_norm[j][None, :],
           "kvn": jnp.pad(b_kv_norm[j], (0, krows - B_KV_RANK), constant_values=1.0)[None, :],
           "wq": _pad_heads(b_w_q_up[j], B_QK, HP).astype(BF16),
           "wk": wk.astype(BF16),
           "qg": jnp.pad(b_q_gain[j], (0, HP - B_QK))[None, :],
           "kg": jnp.pad(b_k_gain[j], (0, HP - B_QK))[None, :]}
    return w, rw, mla, w_out_even[j].astype(BF16)


def kernel(x, c, ctx, c_ctx, norm1_g, norm2_g, mod_w, mod_b, w_in, a_mu, a_w0, a_w2, a_a0, a_a2, a_g2,
           a_k_k, a_k_a, a_r_k, a_ln_w, a_ln_b, b_q_norm, b_w_q_up, b_kv_norm, b_w_kv_up, b_q_gain,
           b_k_gain, w_out_even, w_out_fourier, router_w, router_b, moe_w_gate, moe_w_up, moe_w_down,
           shared_w_gate, shared_w_up, shared_w_down):
    bsz, seq, d = x.shape
    ctx_len = ctx.shape[1]
    assert d == D and seq % TM == 0 and ctx_len % TM == 0 and seq % GRID_W == 0
    assert DEPTH % 2 == 0
    nct = ctx_len // TM
    xs = jnp.concatenate([ctx, x], axis=1)
    n_all = bsz * (ctx_len + seq)
    n_slots = (n_all + N_CLASS * (MB - 1) + MB - 1) // MB * MB
    xs_buf = jnp.zeros((n_slots, D + EXT), F32)

    rows = (bsz + 1 + 7) // 8 * 8
    cs = jnp.zeros((rows, D), F32).at[0:bsz].set(c).at[bsz].set(c_ctx)
    mod = _modulation(cs, mod_w, mod_b)

    def table(part):
        lat = part[0:bsz]
        cx = jnp.broadcast_to(part[bsz][None, :], (bsz, D))
        return jnp.stack([cx, lat], axis=1)[:, :, None, :]

    consts = _rwkv_consts()
    rope = _rope_tables(ctx_len, seq)
    cs_mat, dlat, dctx = _fourier_consts(ctx_len, seq)
    rt = _route_consts(router_w, router_b)
    nctx_chunks = ctx_len // CH

    for i in range(DEPTH):
        j = i // 2
        t0 = nct if i == DEPTH - 1 else 0
        sh1, sc1, g1, sh2, sc2, g2 = [mod[i][:, D * q:D * (q + 1)] for q in range(6)]
        t_sc1 = table((1.0 + sc1) * norm1_g[i][None, :])
        t_sh1 = table(sh1)
        t_g1 = table(g1)
        t_sc2 = table((1.0 + sc2) * norm2_g[i][None, :])
        t_sh2 = table(sh2)
        t_g2 = table(g2)
        if i % 2 == 0:
            w, rw, mla, wo = _even_params(
                j, w_in, a_mu, a_w0, a_w2, a_a0, a_a2, a_g2, a_k_k, a_k_a, a_r_k, a_ln_w, a_ln_b,
                b_q_norm, b_w_q_up, b_kv_norm, b_w_kv_up, b_q_gain, b_k_gain, w_out_even)
            pa, pb = _in_proj(xs, t_sc1, t_sh1, w, nct)
            yf = _rwkv_pass(pa, rw, consts, None, reverse=False, nctx=nctx_chunks)
            ya = _rwkv_pass(pa, rw, consts, yf, reverse=True, nctx=nctx_chunks)
            q, k, v = _mla_prep(pb, mla, rope)
            yb = _attention(q, k, v, ctx_len)
            xs, he, cnt = _even_out(ya, yb, wo, xs, t_g1, t_sc2, t_sh2, rt, nct)
        else:
            hc, hs = _fourier_channels(xs, t_sc1, t_sh1, cs_mat, nct)
            xs, he, cnt = _fourier_out(hc, hs, dlat, dctx, w_out_fourier[j].astype(BF16), xs,
                                       t_g1, t_sc2, t_sh2, rt, ctx_len, t0)
        xs, xs_buf = _moe_layer(
            xs, he, cnt, xs_buf, t_g2,
            moe_w_gate[i].astype(BF16), moe_w_up[i].astype(BF16), moe_w_down[i].astype(BF16),
            shared_w_gate[i].astype(BF16), shared_w_up[i].astype(BF16),
            shared_w_down[i].astype(BF16), nct, t0)
    return xs
```

```python
import functools
import math

import numpy as np
import jax
import jax.numpy as jnp
from jax import lax
from jax.experimental import pallas as pl
from jax.experimental.pallas import tpu as pltpu

F32 = jnp.float32
BF16 = jnp.bfloat16
HI = lax.Precision.HIGHEST

D = 1024
DEPTH = 4
GRID_W = 64
NORM_EPS = 1e-6
A_HEAD = 64
A_HEADS = 8
A_W = 512
A_RANK = 64
A_GATE = 128
A_DECAY_SCALE = 0.606531
A_LN_EPS = 64e-5
A_COLS = 3 * A_W + 4 * A_RANK + A_GATE
B_HEADS = 8
B_NOPE = 64
B_ROPE = 32
B_QK = 96
B_V = 64
B_Q_RANK = 256
B_KV_RANK = 128
B_COLS = B_Q_RANK + B_KV_RANK + B_ROPE
ATTN_SCALE = B_QK ** -0.5
LOG2E = math.log2(math.e)
ROPE_BASE = 10000.0
C_GROUPS = 8
C_GDIM = 128
N_EXPERTS = 16
N_GROUPS = 4
EPG = 4
D_EXPERT = 512
D_SHARED = 512

LANE = 128
TM = 256
CH = 64
RW_NB = 4
ATT_NQ = 4
MB = 256
PERM_ROWS = 1024
PB_COLS = 512
HP = 128
N_CLASS = 24
EXT = 128
PAIRS = ((0, 1), (0, 2), (0, 3), (1, 2), (1, 3), (2, 3))
VMEM_LIMIT = 56 * 1024 * 1024


def _cparams(sem):
    return pltpu.CompilerParams(dimension_semantics=sem, vmem_limit_bytes=VMEM_LIMIT)


def _mm(a, b):
    return jnp.dot(a.astype(BF16), b.astype(BF16), preferred_element_type=F32)


def _mm_nt(a, b):
    return lax.dot_general(a.astype(BF16), b.astype(BF16), (((1,), (1,)), ((), ())),
                           preferred_element_type=F32)


def _mm_tn(a, b):
    return lax.dot_general(a.astype(BF16), b.astype(BF16), (((0,), (0,)), ((), ())),
                           preferred_element_type=F32)


def _mm_hi(a, b):
    return jnp.dot(a, b, precision=HI, preferred_element_type=F32)


def _split_bf16(x):
    hi = x.astype(BF16)
    return hi, (x - hi.astype(F32)).astype(BF16)


def _mm_split_rhs(a, b):
    hi, lo = _split_bf16(b)
    a = a.astype(BF16)
    return jnp.dot(a, hi, preferred_element_type=F32) + jnp.dot(a, lo, preferred_element_type=F32)


def _mm_split_lhs(a, b):
    hi, lo = _split_bf16(a)
    m = a.shape[0]
    r = jnp.dot(jnp.concatenate([hi, lo], axis=0), b.astype(BF16), preferred_element_type=F32)
    return r[0:m] + r[m:2 * m]


def _sigmoid(x):
    return 1.0 / (1.0 + jnp.exp(-x))


def _silu(x):
    return x * _sigmoid(x)


def _rms_mod(x, scale, shift):
    ms = jnp.mean(x * x, axis=-1, keepdims=True)
    return x * lax.rsqrt(ms + NORM_EPS) * scale + shift


def _mod_kernel(c_ref, w_ref, b_ref, o_ref):
    o_ref[...] = _mm_hi(_silu(c_ref[...]), w_ref[...]) + b_ref[...]


def _modulation(cs, mod_w, mod_b):
    rows = cs.shape[0]
    tn = 1536
    return pl.pallas_call(
        _mod_kernel,
        grid=(DEPTH, 6 * D // tn),
        in_specs=[pl.BlockSpec((rows, D), lambda l, n: (0, 0)),
                  pl.BlockSpec((None, D, tn), lambda l, n: (l, 0, n)),
                  pl.BlockSpec((None, 1, tn), lambda l, n: (l, 0, n))],
        out_specs=pl.BlockSpec((None, rows, tn), lambda l, n: (l, 0, n)),
        out_shape=jax.ShapeDtypeStruct((DEPTH, rows, 6 * D), F32),
        compiler_params=_cparams(("arbitrary", "arbitrary")),
        name="modulation",
    )(cs, mod_w, mod_b.reshape(DEPTH, 1, 6 * D))


def _tab_spec(nct, t0=0):
    return pl.BlockSpec((None, None, 1, D),
                        lambda b, t, *_: (b, jnp.minimum((t + t0) // nct, 1), 0, 0))


def _e1_kernel(x_ref, sc_ref, sh_ref, w_ref, pa_ref, pb_ref):
    h = _rms_mod(x_ref[...], sc_ref[...], sh_ref[...])
    p = _mm(h, w_ref[...])
    pa_ref[...] = p[:, :A_COLS]
    pb_ref[...] = p[:, A_COLS:]


def _in_proj(x, sc, sh, w, nct):
    bsz, s, _ = x.shape
    ncol = A_COLS + PB_COLS
    return pl.pallas_call(
        _e1_kernel,
        grid=(bsz, s // TM),
        in_specs=[pl.BlockSpec((None, TM, D), lambda b, t: (b, t, 0)),
                  _tab_spec(nct), _tab_spec(nct),
                  pl.BlockSpec((D, ncol), lambda b, t: (0, 0))],
        out_specs=[pl.BlockSpec((None, TM, A_COLS), lambda b, t: (b, t, 0)),
                   pl.BlockSpec((None, TM, PB_COLS), lambda b, t: (b, t, 0))],
        out_shape=[jax.ShapeDtypeStruct((bsz, s, A_COLS), F32),
                   jax.ShapeDtypeStruct((bsz, s, PB_COLS), F32)],
        compiler_params=_cparams(("parallel", "parallel")),
        name="in_proj",
    )(x, sc, sh, w)


def _chunk_of(i, nctx, nch, reverse):
    if not reverse:
        return i
    return jnp.where(i < nctx, nctx - 1 - i, nch - 1 - (i - nctx))


def _rw_kernel(*refs, reverse, final, nctx, nch, nb):
    if final:
        (pc_ref, pp_ref, pn_ref, mu_ref, w0_ref, w2_ref, a0_ref, a2_ref, kk_ref, ka_ref,
         tri_ref, ms_ref, mi_ref, bones_ref,
         a0o_ref, a2o_ref, g2_ref, rk_ref, lnw_ref, lnb_ref, yf_ref, out_ref, z_ref) = refs
    else:
        (pc_ref, pp_ref, pn_ref, mu_ref, w0_ref, w2_ref, a0_ref, a2_ref, kk_ref, ka_ref,
         tri_ref, ms_ref, mi_ref, bones_ref, out_ref, z_ref) = refs
    i = pl.program_id(1)
    c = _chunk_of(i, nctx, nch, reverse)

    @pl.when(i == 0)
    def _():
        z_ref[...] = jnp.zeros_like(z_ref)

    seg_first = jnp.logical_or(c == 0, c == nctx)
    seg_last = jnp.logical_or(c == nctx - 1, c == nch - 1)
    rid = lax.broadcasted_iota(jnp.int32, (CH, 1), 0)
    bones = bones_ref[...]
    lane = lax.broadcasted_iota(jnp.int32, (1, LANE), 1)
    m0 = (lane < A_HEAD).astype(F32)
    m1 = 1.0 - m0
    eye = (lax.broadcasted_iota(jnp.int32, (LANE, LANE), 0)
           == lax.broadcasted_iota(jnp.int32, (LANE, LANE), 1))
    eye_c = rid == (lane & (A_HEAD - 1))
    mstrict = ms_ref[...] > 0.5
    mincl = mi_ref[...] > 0.5
    h0 = lane < A_HEAD
    h0w = jnp.concatenate([h0, h0], axis=1)
    zero_b = jnp.zeros((), BF16)

    def stack(x):
        x = x.astype(BF16)
        sel = h0 if x.shape[1] == LANE else h0w
        return jnp.concatenate([jnp.where(sel, x, zero_b), jnp.where(sel, zero_b, x)], axis=0)

    pre = []
    chains = []
    for n in range(nb):
        p = pc_ref[n]
        prow = jnp.where(seg_first, 0.0, pp_ref[n, 7:8, :])
        nrow = jnp.where(seg_last, 0.0, pn_ref[n, 0:1, :])
        pprev = jnp.where(rid == 0, prow, pltpu.roll(p, 1, 0))
        pnext = jnp.where(rid == CH - 1, nrow, pltpu.roll(p, CH - 1, 0))
        ps = p + mu_ref[...] * (0.5 * (pprev + pnext) - p)
        r = ps[:, 0:A_W]
        k = ps[:, A_W:2 * A_W]
        v = ps[:, 2 * A_W:3 * A_W]
        dd = ps[:, 3 * A_W:3 * A_W + 128]
        ad = ps[:, 3 * A_W + 128:3 * A_W + 256]
        gd = ps[:, 3 * A_W + 256:3 * A_W + 384]
        logw = -A_DECAY_SCALE * _sigmoid(w0_ref[...] + _mm(jnp.tanh(dd), w2_ref[...]))
        aic = _sigmoid(a0_ref[...] + _mm(ad, a2_ref[...]))
        kkr = k * kk_ref[...]
        kk = kkr * lax.rsqrt(_mm_split_lhs(kkr * kkr, bones) + 1e-12)
        kd = k * (1.0 + (aic - 1.0) * ka_ref[...])
        bd = kk * aic
        cum = _mm_split_rhs(tri_ref[...], logw)
        tot = cum[0:1, :] if reverse else cum[CH - 1:CH, :]
        rt = r * jnp.exp(cum)
        at = -kk * jnp.exp(cum - logw)
        ei = jnp.exp(-cum)
        kt = kd * ei
        bt = bd * ei
        et = jnp.exp(tot - cum)
        kh = kd * et
        bh = bd * et
        gam = jnp.exp(tot)
        pre.append((r, k, v, ad, gd, kd))
        for j in range(A_HEADS // 2):
            sl = slice(LANE * j, LANE * (j + 1))
            chains.append({
                "n": n, "j": j, "gam": gam[:, sl], "a": at[:, sl], "r": rt[:, sl],
                "bts": stack(bt[:, sl]), "kts": stack(kt[:, sl]), "vs": stack(v[:, sl]),
                "bhs": stack(bh[:, sl]), "khs": stack(kh[:, sl])})

    for ch in chains:
        big = _mm_nt(jnp.concatenate([ch["a"], ch["r"]], axis=0),
                     jnp.concatenate([ch["bts"], ch["kts"]], axis=0))
        ch["nm"] = jnp.where(mstrict, big[0:CH, 0:128], 0.0)
        aak = jnp.where(mstrict, big[0:CH, 128:256], 0.0)
        ch["arb"] = jnp.where(mincl, big[CH:2 * CH, 0:128], 0.0)
        ark = jnp.where(mincl, big[CH:2 * CH, 128:256], 0.0)
        ch["aa"] = jnp.concatenate([aak, ark], axis=0)
    for ch in chains:
        avs = _mm(ch["aa"], ch["vs"])
        ch["rhs"] = jnp.concatenate([ch["a"], avs[0:CH]], axis=1)
        ch["av2"] = avs[CH:2 * CH]
        ch["t"] = jnp.where(eye_c, 1.0, ch["nm"])
    for it in range(5):
        for ch in chains:
            n2 = _mm(ch["nm"], stack(ch["nm"]))
            ch["t"] = ch["t"] + _mm(n2, stack(ch["t"]))
            ch["nm"] = n2
    for ch in chains:
        ch["x"] = stack(_mm(ch["t"], stack(ch["rhs"])))
    ys = [[None] * (A_HEADS // 2) for _ in range(nb)]
    for ch in chains:
        x = ch["x"]
        qy = jnp.concatenate([ch["r"], ch["av2"]], axis=1) + _mm(ch["arb"], x)
        z = z_ref[ch["n"], ch["j"]]
        ys[ch["n"]][ch["j"]] = qy[:, 128:256] + _mm(qy[:, 0:128], z)
        lhs = jnp.concatenate([ch["bhs"], ch["khs"]], axis=0)
        rhs = jnp.concatenate(
            [x, jnp.concatenate([jnp.zeros((128, 128), BF16), ch["vs"]], axis=1)], axis=0)
        m = _mm_tn(lhs, rhs)
        gt = jnp.where(eye, ch["gam"], 0.0) + m[:, 0:128]
        z_ref[ch["n"], ch["j"]] = _mm(gt, z) + m[:, 128:256]

    for n in range(nb):
        y = jnp.concatenate(ys[n], axis=1)
        if not final:
            out_ref[n] = y
        else:
            r, k, v, ad, gd, kd = pre[n]
            y = y + yf_ref[n]
            aico = _sigmoid(a0o_ref[...] + _mm(ad, a2o_ref[...]))
            kdo = k * (1.0 + (aico - 1.0) * ka_ref[...])
            sums = _mm_split_lhs(jnp.concatenate([y, r * rk_ref[...] * (kd + kdo)], axis=0), bones)
            yc = y - sums[0:CH] * (1.0 / A_HEAD)
            var = _mm_split_lhs(yc * yc, bones) * (1.0 / A_HEAD)
            yn = yc * lax.rsqrt(var + A_LN_EPS) * lnw_ref[...] + lnb_ref[...]
            bonus = sums[CH:2 * CH] * v
            g = _mm(_sigmoid(gd), g2_ref[...])
            out_ref[n] = ((yn + bonus) * g).astype(out_ref.dtype)


def _rwkv_pass(pa, prm, consts, yf, *, reverse, nctx):
    bsz, s, _ = pa.shape
    nch = s // CH
    final = yf is not None
    d = 1 if reverse else 0
    nb = RW_NB
    assert bsz % nb == 0

    def cmap(b, i):
        return (b, _chunk_of(i, nctx, nch, reverse), 0)

    def pmap(b, i):
        return (b, jnp.maximum(_chunk_of(i, nctx, nch, reverse) * (CH // 8) - 1, 0), 0)

    def nmap(b, i):
        return (b, jnp.minimum((_chunk_of(i, nctx, nch, reverse) + 1) * (CH // 8), s // 8 - 1), 0)

    def full(a):
        return pl.BlockSpec(a.shape, lambda b, i: (0,) * a.ndim)

    args = [pa, pa, pa, prm["mu"], prm["w0"][d], prm["w2"][d], prm["a0"][d], prm["a2"][d],
            prm["kk"], prm["ka"], consts["tri"][d], consts["mstrict"][d], consts["mincl"][d],
            consts["bones"]]
    specs = [pl.BlockSpec((nb, CH, A_COLS), cmap),
             pl.BlockSpec((nb, 8, A_COLS), pmap),
             pl.BlockSpec((nb, 8, A_COLS), nmap)] + [full(a) for a in args[3:]]
    if final:
        extra = [prm["a0"][1 - d], prm["a2"][1 - d], prm["g2"], prm["rk"], prm["lnw"], prm["lnb"]]
        args += extra + [yf]
        specs += [full(a) for a in extra] + [pl.BlockSpec((nb, CH, A_W), cmap)]
    return pl.pallas_call(
        functools.partial(_rw_kernel, reverse=reverse, final=final, nctx=nctx, nch=nch, nb=nb),
        grid=(bsz // nb, nch),
        in_specs=specs,
        out_specs=pl.BlockSpec((nb, CH, A_W), cmap),
        out_shape=jax.ShapeDtypeStruct((bsz, s, A_W), BF16 if final else F32),
        scratch_shapes=[pltpu.VMEM((nb, A_HEADS // 2, LANE, LANE), F32)],
        compiler_params=_cparams(("parallel", "arbitrary")),
        name="rwkv_bwd" if reverse else "rwkv_fwd",
    )(*args)


def _mla_prep_kernel(pb_ref, qn_ref, kvn_ref, wq_ref, wk_ref, qg_ref, kg_ref, rp_ref,
                     c_ref, s1_ref, q_ref, k_ref, v_ref):
    pb = pb_ref[...]
    pq = pb[:, 0:B_Q_RANK]
    xq = pq * lax.rsqrt(jnp.mean(pq * pq, axis=-1, keepdims=True) + NORM_EPS) * qn_ref[...]
    q = _mm(xq, wq_ref[...])
    pk = pb[:, B_Q_RANK:PB_COLS]
    is_kv = lax.broadcasted_iota(jnp.int32, (1, PB_COLS - B_Q_RANK), 1) < B_KV_RANK
    ms = jnp.sum(jnp.where(is_kv, pk * pk, 0.0), axis=-1, keepdims=True) * (1.0 / B_KV_RANK)
    xk = jnp.where(is_kv, pk * lax.rsqrt(ms + NORM_EPS) * kvn_ref[...], pk)
    kv = _mm(xk, wk_ref[...])
    cc, s1 = c_ref[...], s1_ref[...]

    def head(xh, gain):
        ss = jnp.sum(xh * xh, axis=-1, keepdims=True) * (1.0 / B_QK)
        xh = xh * lax.rsqrt(ss + NORM_EPS) * gain
        return xh * cc + _mm(xh, rp_ref[...]) * s1

    for h in range(B_HEADS):
        sl = slice(HP * h, HP * (h + 1))
        q_ref[:, sl] = (head(q[:, sl], qg_ref[...]) * (ATTN_SCALE * LOG2E)).astype(BF16)
        k_ref[:, sl] = head(kv[:, sl], kg_ref[...]).astype(BF16)
    v_ref[...] = kv[:, B_HEADS * HP:].astype(BF16)


def _mla_prep(pb, prm, rope):
    bsz, s, _ = pb.shape

    def full(a):
        return pl.BlockSpec(a.shape, lambda b, t: (0,) * a.ndim)

    consts = [prm["qn"], prm["kvn"], prm["wq"], prm["wk"], prm["qg"], prm["kg"], rope[0]]
    rspec = pl.BlockSpec((TM, HP), lambda b, t: (t, 0))
    return pl.pallas_call(
        _mla_prep_kernel,
        grid=(bsz, s // TM),
        in_specs=[pl.BlockSpec((None, TM, PB_COLS), lambda b, t: (b, t, 0))]
        + [full(a) for a in consts] + [rspec, rspec],
        out_specs=[pl.BlockSpec((None, TM, B_HEADS * HP), lambda b, t: (b, t, 0)),
                   pl.BlockSpec((None, TM, B_HEADS * HP), lambda b, t: (b, t, 0)),
                   pl.BlockSpec((None, TM, B_HEADS * B_V), lambda b, t: (b, t, 0))],
        out_shape=[jax.ShapeDtypeStruct((bsz, s, B_HEADS * HP), BF16),
                   jax.ShapeDtypeStruct((bsz, s, B_HEADS * HP), BF16),
                   jax.ShapeDtypeStruct((bsz, s, B_HEADS * B_V), BF16)],
        compiler_params=_cparams(("parallel", "parallel")),
        name="mla_prep",
    )(pb, *consts, *rope[1:])


def _att_kernel(*refs, nq):
    q_refs, (k_ref, v_ref, o_ref) = refs[:nq], refs[nq:]
    lane = lax.broadcasted_iota(jnp.int32, (1, 2 * B_V), 1)
    outs = []
    for s in range(2):
        q = jnp.concatenate([r[:, HP * s:HP * (s + 1)] for r in q_refs], axis=0)
        k = k_ref[:, HP * s:HP * (s + 1)]
        sc = lax.dot_general(q, k, (((1,), (1,)), ((), ())), preferred_element_type=F32)
        p = jnp.exp2(sc - jnp.max(sc, axis=-1, keepdims=True)).astype(BF16)
        ones_lane = B_V * (1 - s)
        vm = jnp.where((lane >= B_V * s) & (lane < B_V * (s + 1)), v_ref[...],
                       jnp.where(lane == ones_lane, 1.0, 0.0).astype(BF16))
        acc = jnp.dot(p, vm, preferred_element_type=F32)
        outs.append(acc / acc[:, ones_lane:ones_lane + 1])
    o_ref[...] = jnp.where(lane < B_V, outs[0], outs[1]).astype(BF16)


def _attention(q, k, v, ctx_len):
    bsz, s, _ = q.shape
    nct = ctx_len // TM
    nlat = (s - ctx_len) // TM
    nq = ATT_NQ if nlat % ATT_NQ == 0 else 1

    def call(nq, nk, tiles, t0, name):
        q_specs = [pl.BlockSpec((None, TM, 2 * HP), lambda b, j, t, i=i: (b, t0 + nq * t + i, j))
                   for i in range(nq)]
        return pl.pallas_call(
            functools.partial(_att_kernel, nq=nq),
            grid=(bsz, B_HEADS // 2, tiles // nq),
            in_specs=q_specs + [pl.BlockSpec((None, nk, 2 * HP), lambda b, j, t: (b, 0, j)),
                                pl.BlockSpec((None, nk, 2 * B_V), lambda b, j, t: (b, 0, j))],
            out_specs=pl.BlockSpec((None, nq * TM, 2 * B_V), lambda b, j, t: (b, t, j)),
            out_shape=jax.ShapeDtypeStruct((bsz, tiles * TM, B_HEADS * B_V), BF16),
            compiler_params=_cparams(("parallel", "parallel", "parallel")),
            name=name,
        )(*([q] * nq), k, v)

    return call(1, ctx_len, nct, 0, "mla_attention_ctx"), call(nq, s, nlat, nct, "mla_attention")


def _route_tail(x, o, g1, sc2, sh2, rwh_ref, rwl_ref, rb_ref, pm_ref, pab_ref, stril_ref,
                xo_ref, he_ref, cnt_ref, cnt_scr, first):
    xn = x + g1 * o
    xo_ref[...] = xn
    h2 = _rms_mod(xn, sc2, sh2)
    he_ref[:, 0:D] = h2

    @pl.when(first)
    def _():
        cnt_scr[...] = jnp.zeros_like(cnt_scr)

    lane = lax.broadcasted_iota(jnp.int32, (1, LANE), 1)
    h_hi, h_lo = _split_bf16(h2)
    r_hi, r_lo = rwh_ref[...], rwl_ref[...]
    logits = (jnp.dot(h_hi, r_hi, preferred_element_type=F32)
              + jnp.dot(h_lo, r_hi, preferred_element_type=F32)
              + jnp.dot(h_hi, r_lo, preferred_element_type=F32))
    scores = _sigmoid(logits)
    sel = scores + rb_ref[...]
    ps = jnp.where(lane < N_CLASS, _mm_split_lhs(sel, pm_ref[...]), -jnp.inf)
    best = jnp.max(ps, axis=-1, keepdims=True)
    cls = jnp.min(jnp.where(ps == best, lane, LANE), axis=-1, keepdims=True)
    onehot = (lane == cls).astype(F32)
    s_hi, s_lo = _split_bf16(scores)
    sab = (jnp.dot(s_hi, pab_ref[...], preferred_element_type=F32)
           + jnp.dot(s_lo, pab_ref[...], preferred_element_type=F32))
    sa = jnp.sum(onehot * sab[:, 0:LANE], axis=-1, keepdims=True)
    sb = jnp.sum(onehot * sab[:, LANE:2 * LANE], axis=-1, keepdims=True)
    den = sa + sb
    prefix = jnp.dot(stril_ref[...], onehot.astype(BF16), preferred_element_type=F32)
    rank = jnp.sum(onehot * (prefix + cnt_scr[0:1, :]), axis=-1, keepdims=True)
    cnt_scr[...] = cnt_scr[...] + jnp.sum(onehot, axis=0, keepdims=True)
    cnt_ref[...] = cnt_scr[...]
    he_ref[:, D:D + EXT] = jnp.where(
        lane == 0, sa / den,
        jnp.where(lane == 1, sb / den,
                  jnp.where(lane == 2, cls.astype(F32), jnp.where(lane == 3, rank, 0.0))))


def _tail_out(bsz, s_out):
    specs = [pl.BlockSpec((None, TM, D), lambda b, t: (b, t, 0)),
             pl.BlockSpec((None, TM, D + EXT), lambda b, t: (b, t, 0)),
             pl.BlockSpec((8, LANE), lambda b, t: (0, 0))]
    shapes = [jax.ShapeDtypeStruct((bsz, s_out, D), F32),
              jax.ShapeDtypeStruct((bsz, s_out, D + EXT), F32),
              jax.ShapeDtypeStruct((8, LANE), F32)]
    return specs, shapes


def _e2_kernel(ya_ref, ybc_ref, ybl_ref, wo_ref, x_ref, g1_ref, sc_ref, sh_ref, rwh_ref, rwl_ref, rb_ref,
               pm_ref, pab_ref, stril_ref, xo_ref, he_ref, cnt_ref, cnt_scr, *, nct):
    yb = jnp.where(pl.program_id(1) < nct, ybc_ref[...], ybl_ref[...])
    o = (jnp.dot(ya_ref[...], wo_ref[0:A_W, :], preferred_element_type=F32)
         + jnp.dot(yb, wo_ref[A_W:2 * A_W, :], preferred_element_type=F32))
    first = jnp.logical_and(pl.program_id(0) == 0, pl.program_id(1) == 0)
    _route_tail(x_ref[...], o, g1_ref[...], sc_ref[...], sh_ref[...], rwh_ref, rwl_ref, rb_ref, pm_ref,
                pab_ref, stril_ref, xo_ref, he_ref, cnt_ref, cnt_scr, first)


def _even_out(ya, yb, wo, x, g1, sc2, sh2, rt, nct):
    bsz, s, _ = x.shape
    out_specs, out_shape = _tail_out(bsz, s)
    yb_ctx, yb_lat = yb

    def full(a):
        return pl.BlockSpec(a.shape, lambda b, t: (0,) * a.ndim)

    return pl.pallas_call(
        functools.partial(_e2_kernel, nct=nct),
        grid=(bsz, s // TM),
        in_specs=[pl.BlockSpec((None, TM, A_W), lambda b, t: (b, t, 0)),
                  pl.BlockSpec((None, TM, A_W), lambda b, t: (b, jnp.minimum(t, nct - 1), 0)),
                  pl.BlockSpec((None, TM, A_W), lambda b, t: (b, jnp.maximum(t - nct, 0), 0)),
                  full(wo),
                  pl.BlockSpec((None, TM, D), lambda b, t: (b, t, 0)),
                  _tab_spec(nct), _tab_spec(nct), _tab_spec(nct)] + [full(a) for a in rt],
        out_specs=out_specs,
        out_shape=out_shape,
        scratch_shapes=[pltpu.VMEM((8, LANE), F32)],
        compiler_params=_cparams(("arbitrary", "arbitrary")),
        name="even_out_route",
    )(ya, yb_ctx, yb_lat, wo, x, g1, sc2, sh2, *rt)


def _f1_kernel(x_ref, sc_ref, sh_ref, cs_ref, hc_ref, hs_ref):
    h = _rms_mod(x_ref[...], sc_ref[...], sh_ref[...]).astype(BF16)
    for g in range(C_GROUPS):
        sl = slice(C_GDIM * g, C_GDIM * (g + 1))
        r = jnp.dot(h[:, sl], cs_ref[...], preferred_element_type=F32)
        hc_ref[:, sl] = r[:, 0:C_GDIM].astype(BF16)
        hs_ref[:, sl] = r[:, C_GDIM:2 * C_GDIM].astype(BF16)


def _fourier_channels(x, sc, sh, cs, nct):
    bsz, s, _ = x.shape
    return pl.pallas_call(
        _f1_kernel,
        grid=(bsz, s // TM),
        in_specs=[pl.BlockSpec((None, TM, D), lambda b, t: (b, t, 0)),
                  _tab_spec(nct), _tab_spec(nct),
                  pl.BlockSpec(cs.shape, lambda b, t: (0, 0))],
        out_specs=[pl.BlockSpec((None, TM, D), lambda b, t: (b, t, 0)),
                   pl.BlockSpec((None, TM, D), lambda b, t: (b, t, 0))],
        out_shape=[jax.ShapeDtypeStruct((bsz, s, D), BF16),
                   jax.ShapeDtypeStruct((bsz, s, D), BF16)],
        compiler_params=_cparams(("parallel", "parallel")),
        name="fourier_channels",
    )(x, sc, sh, cs)


def _f2_kernel(hc_ref, hs_ref, dl_ref, dc_ref, wo_ref, x_ref, g1_ref, sc_ref, sh_ref, rwh_ref, rwl_ref,
               rb_ref, pm_ref, pab_ref, stril_ref, xo_ref, he_ref, cnt_ref, cnt_scr, f_scr,
               *, ctx_len, nct, t0):
    t = pl.program_id(1) + t0
    s = hc_ref.shape[0]
    seq = s - ctx_len

    if t0 < nct:
        @pl.when(t < nct)
        def _():
            f_scr[...] = (jnp.dot(dc_ref[:, 0:ctx_len], hc_ref[0:ctx_len, :], preferred_element_type=F32)
                          + jnp.dot(dc_ref[:, ctx_len:2 * ctx_len], hs_ref[0:ctx_len, :],
                                    preferred_element_type=F32))

    @pl.when(t >= nct)
    def _():
        f_scr[...] = (jnp.dot(dl_ref[:, 0:seq], hc_ref[ctx_len:s, :], preferred_element_type=F32)
                      + jnp.dot(dl_ref[:, seq:2 * seq], hs_ref[ctx_len:s, :],
                                preferred_element_type=F32))

    o = _mm(f_scr[...], wo_ref[...])
    first = jnp.logical_and(pl.program_id(0) == 0, pl.program_id(1) == 0)
    _route_tail(x_ref[...], o, g1_ref[...], sc_ref[...], sh_ref[...], rwh_ref, rwl_ref, rb_ref, pm_ref,
                pab_ref, stril_ref, xo_ref, he_ref, cnt_ref, cnt_scr, first)


def _fourier_out(hc, hs, dlat, dctx, wo, x, g1, sc2, sh2, rt, ctx_len, t0):
    bsz, s, _ = x.shape
    nct = ctx_len // TM
    nt = s // TM - t0
    out_specs, out_shape = _tail_out(bsz, nt * TM)

    def full(a):
        return pl.BlockSpec(a.shape, lambda b, t: (0,) * a.ndim)

    return pl.pallas_call(
        functools.partial(_f2_kernel, ctx_len=ctx_len, nct=nct, t0=t0),
        grid=(bsz, nt),
        in_specs=[pl.BlockSpec((None, s, D), lambda b, t: (b, 0, 0)),
                  pl.BlockSpec((None, s, D), lambda b, t: (b, 0, 0)),
                  pl.BlockSpec((TM, dlat.shape[1]), lambda b, t: (jnp.maximum(t + t0 - nct, 0), 0)),
                  pl.BlockSpec((TM, dctx.shape[1]), lambda b, t: (jnp.minimum(t + t0, nct - 1), 0)),
                  full(wo),
                  pl.BlockSpec((None, TM, D), lambda b, t: (b, t + t0, 0)),
                  _tab_spec(nct, t0), _tab_spec(nct, t0), _tab_spec(nct, t0)]
        + [full(a) for a in rt],
        out_specs=out_specs,
        out_shape=out_shape,
        scratch_shapes=[pltpu.VMEM((8, LANE), F32), pltpu.VMEM((TM, D), F32)],
        compiler_params=_cparams(("arbitrary", "arbitrary")),
        name="fourier_out_route",
    )(hc, hs, dlat, dctx, wo, x, g1, sc2, sh2, *rt)


def _row_copy(src_ref, src_row, dst_ref, dst_row, sem):
    return pltpu.make_async_copy(src_ref.at[pl.ds(src_row, 1)], dst_ref.at[pl.ds(dst_row, 1)], sem)


def _scatter_kernel(slot_ref, h_ref, xin_ref, xs_ref, sem, *, rows):
    del xin_ref
    base = pl.program_id(0) * rows

    def issue(g, carry):
        for u in range(8):
            r = g * 8 + u
            _row_copy(h_ref, r, xs_ref, slot_ref[base + r], sem).start(priority=u % 2)
        return carry

    lax.fori_loop(0, rows // 8, issue, 0)
    pltpu.make_async_copy(h_ref, xs_ref.at[pl.ds(0, rows)], sem).wait()


def _scatter_rows(slot, src, dst):
    n = src.shape[0]
    rows = math.gcd(n, PERM_ROWS)
    return pl.pallas_call(
        functools.partial(_scatter_kernel, rows=rows),
        grid_spec=pltpu.PrefetchScalarGridSpec(
            num_scalar_prefetch=1,
            grid=(n // rows,),
            in_specs=[pl.BlockSpec((rows, src.shape[1]), lambda i, sl: (i, 0)),
                      pl.BlockSpec(memory_space=pl.ANY)],
            out_specs=pl.BlockSpec(memory_space=pl.ANY),
            scratch_shapes=[pltpu.SemaphoreType.DMA(())]),
        out_shape=jax.ShapeDtypeStruct(dst.shape, dst.dtype),
        input_output_aliases={2: 0},
        compiler_params=_cparams(("arbitrary",)),
        name="moe_scatter_rows",
    )(slot, src, dst)


def _moe_kernel(ea_ref, eb_ref, ok_ref, xs_ref, wga_ref, wgb_ref, wua_ref, wub_ref, wda_ref, wdb_ref,
                y_ref):
    i = pl.program_id(0)

    @pl.when(ok_ref[i] > 0)
    def _():
        xe = xs_ref[...]
        x = xe[:, 0:D].astype(BF16)
        wa = xe[:, D:D + 1]
        wb = xe[:, D + 1:D + 2]
        ha = _silu(jnp.dot(x, wga_ref[...], preferred_element_type=F32)) \
            * jnp.dot(x, wua_ref[...], preferred_element_type=F32) * wa
        hb = _silu(jnp.dot(x, wgb_ref[...], preferred_element_type=F32)) \
            * jnp.dot(x, wub_ref[...], preferred_element_type=F32) * wb
        y_ref[...] = (jnp.dot(ha.astype(BF16), wda_ref[...], preferred_element_type=F32)
                      + jnp.dot(hb.astype(BF16), wdb_ref[...], preferred_element_type=F32))

    @pl.when(ok_ref[i] == 0)
    def _():
        y_ref[...] = jnp.zeros_like(y_ref)


def _moe_blocks(ea, eb, ok, xs, wg, wu, wd):
    n_slots = xs.shape[0]
    wa_spec = pl.BlockSpec((None, D, D_EXPERT), lambda i, ea, eb, ok: (ea[i], 0, 0))
    wb_spec = pl.BlockSpec((None, D, D_EXPERT), lambda i, ea, eb, ok: (eb[i], 0, 0))
    da_spec = pl.BlockSpec((None, D_EXPERT, D), lambda i, ea, eb, ok: (ea[i], 0, 0))
    db_spec = pl.BlockSpec((None, D_EXPERT, D), lambda i, ea, eb, ok: (eb[i], 0, 0))
    return pl.pallas_call(
        _moe_kernel,
        grid_spec=pltpu.PrefetchScalarGridSpec(
            num_scalar_prefetch=3,
            grid=(n_slots // MB,),
            in_specs=[pl.BlockSpec((MB, D + EXT), lambda i, ea, eb, ok: (i, 0)),
                      wa_spec, wb_spec, wa_spec, wb_spec, da_spec, db_spec],
            out_specs=pl.BlockSpec((MB, D), lambda i, ea, eb, ok: (i, 0))),
        out_shape=jax.ShapeDtypeStruct((n_slots, D), F32),
        compiler_params=_cparams(("arbitrary",)),
        name="moe_experts",
    )(ea, eb, ok, xs, wg, wg, wu, wu, wd, wd)


def _comb_kernel(slot_ref, x_ref, he_ref, g2_ref, wg_ref, wu_ref, wd_ref, y_ref, o_ref, ybuf, sem,
                 *, nt, nsteps):
    step = pl.program_id(0) * nt + pl.program_id(1)
    cur = step % 2

    def gather(tile, buf):
        base = tile * TM

        for r in range(TM):
            _row_copy(y_ref, slot_ref[base + r], ybuf.at[buf], r, sem.at[buf]).start(priority=r % 2)

    @pl.when(step == 0)
    def _():
        gather(0, 0)

    gather(jnp.minimum(step + 1, nsteps - 1), 1 - cur)
    h = he_ref[:, 0:D].astype(BF16)
    hid = _silu(jnp.dot(h, wg_ref[...], preferred_element_type=F32)) \
        * jnp.dot(h, wu_ref[...], preferred_element_type=F32)
    shared = jnp.dot(hid.astype(BF16), wd_ref[...], preferred_element_type=F32)
    pltpu.make_async_copy(y_ref.at[pl.ds(0, TM)], ybuf.at[cur], sem.at[cur]).wait()
    o_ref[...] = x_ref[...] + g2_ref[...] * (shared + ybuf[cur])

    @pl.when(step == nsteps - 1)
    def _():
        pltpu.make_async_copy(y_ref.at[pl.ds(0, TM)], ybuf.at[1 - cur], sem.at[1 - cur]).wait()


def _combine(slot, x, he, y, g2, wg, wu, wd, nct, t0):
    bsz, s, _ = x.shape
    nt = s // TM

    def full(a):
        return pl.BlockSpec(a.shape, lambda b, t, sl: (0,) * a.ndim)

    return pl.pallas_call(
        functools.partial(_comb_kernel, nt=nt, nsteps=bsz * nt),
        grid_spec=pltpu.PrefetchScalarGridSpec(
            num_scalar_prefetch=1,
            grid=(bsz, nt),
            in_specs=[pl.BlockSpec((None, TM, D), lambda b, t, sl: (b, t, 0)),
                      pl.BlockSpec((None, TM, D + EXT), lambda b, t, sl: (b, t, 0)),
                      _tab_spec(nct, t0), full(wg), full(wu), full(wd),
                      pl.BlockSpec(memory_space=pl.ANY)],
            out_specs=pl.BlockSpec((None, TM, D), lambda b, t, sl: (b, t, 0)),
            scratch_shapes=[pltpu.VMEM((2, TM, D), F32), pltpu.SemaphoreType.DMA((2,))]),
        out_shape=jax.ShapeDtypeStruct((bsz, s, D), F32),
        compiler_params=_cparams(("arbitrary", "arbitrary")),
        name="moe_combine",
    )(slot, x, he, g2, wg, wu, wd, y)


def _moe_layer(x, he, cnt, xs_buf, g2, wg, wu, wd, swg, swu, swd, nct, t0):
    bsz, s, _ = x.shape
    n = bsz * s
    n_slots = xs_buf.shape[0]
    cls = he[:, :, D + 2].reshape(n).astype(jnp.int32)
    rank = he[:, :, D + 3].reshape(n).astype(jnp.int32)
    counts = cnt[0, 0:N_CLASS].astype(jnp.int32)
    padded = (counts + MB - 1) // MB * MB
    pend = jnp.cumsum(padded)
    slot = (pend - padded)[cls] + rank
    bstart = jnp.arange(n_slots // MB, dtype=jnp.int32) * MB
    bcls = jnp.minimum(jnp.sum((bstart[:, None] >= pend[None, :]).astype(jnp.int32), axis=1),
                       N_CLASS - 1)
    ok = (bstart < pend[-1]).astype(jnp.int32)
    pa = jnp.asarray([p[0] for p in PAIRS], jnp.int32)
    pb = jnp.asarray([p[1] for p in PAIRS], jnp.int32)
    ea = EPG * (bcls // len(PAIRS)) + pa[bcls % len(PAIRS)]
    eb = EPG * (bcls // len(PAIRS)) + pb[bcls % len(PAIRS)]
    xs_buf = _scatter_rows(slot, he.reshape(n, D + EXT), xs_buf)
    y = _moe_blocks(ea, eb, ok, xs_buf, wg, wu, wd)
    return _combine(slot, x, he, y, g2, swg, swu, swd, nct, t0), xs_buf


def _rwkv_consts():
    t = np.arange(CH)
    low = (t[:, None] >= t[None, :]).astype(np.float32)
    tri = np.stack([low, low.T])
    t2 = np.arange(2 * CH) % CH
    mstrict = np.stack([t[:, None] > t2[None, :], t[:, None] < t2[None, :]]).astype(np.float32)
    mincl = np.stack([t[:, None] >= t2[None, :], t[:, None] <= t2[None, :]]).astype(np.float32)
    bones = np.kron(np.eye(A_HEADS, dtype=np.float32), np.ones((A_HEAD, A_HEAD), np.float32))
    return {"tri": jnp.asarray(tri), "mstrict": jnp.asarray(mstrict), "mincl": jnp.asarray(mincl),
            "bones": jnp.asarray(bones)}


def _route_consts(router_w, router_b):
    pm = np.zeros((LANE, LANE), np.float32)
    pa = np.zeros((LANE, LANE), np.float32)
    pb = np.zeros((LANE, LANE), np.float32)
    for cidx in range(N_CLASS):
        g, (a, b) = cidx // len(PAIRS), PAIRS[cidx % len(PAIRS)]
        pm[EPG * g + a, cidx] = 1.0
        pm[EPG * g + b, cidx] = 1.0
        pa[EPG * g + a, cidx] = 1.0
        pb[EPG * g + b, cidx] = 1.0
    tt = np.arange(TM)
    stril = (tt[:, None] > tt[None, :]).astype(np.float32)
    rw = jnp.pad(router_w, ((0, 0), (0, LANE - N_EXPERTS)))
    rw_hi = rw.astype(BF16)
    rw_lo = (rw - rw_hi.astype(F32)).astype(BF16)
    rb = jnp.pad(router_b, (0, LANE - N_EXPERTS))[None, :]
    pab = np.concatenate([pa, pb], axis=1)
    return [rw_hi, rw_lo, rb, jnp.asarray(pm, BF16), jnp.asarray(pab, BF16), jnp.asarray(stril, BF16)]


def _rope_tables(ctx_len, seq):
    rows = seq // GRID_W
    row = np.repeat(np.arange(rows), GRID_W).astype(np.float32)
    col = np.tile(np.arange(GRID_W), rows).astype(np.float32)
    half = B_ROPE // 2
    inv = (ROPE_BASE ** (-np.arange(0, half, 2, dtype=np.float32) / half)).astype(np.float32)
    ang = np.concatenate([row[:, None] * inv, col[:, None] * inv], axis=-1).astype(np.float32)
    cos = np.concatenate([np.ones((ctx_len, half), np.float32), np.cos(ang)], axis=0)
    sin = np.concatenate([np.zeros((ctx_len, half), np.float32), np.sin(ang)], axis=0)
    s = ctx_len + seq
    cc = np.zeros((s, HP), np.float32)
    ss = np.zeros((s, HP), np.float32)
    cc[:, 0:B_NOPE] = 1.0
    cc[:, B_NOPE:B_NOPE + half] = cos
    cc[:, B_NOPE + half:B_QK] = cos
    ss[:, B_NOPE:B_NOPE + half] = -sin
    ss[:, B_NOPE + half:B_QK] = sin
    partner = np.zeros((HP, HP), np.float32)
    i = np.arange(half)
    partner[B_NOPE + half + i, B_NOPE + i] = 1.0
    partner[B_NOPE + i, B_NOPE + half + i] = 1.0
    return jnp.asarray(partner, BF16), jnp.asarray(cc), jnp.asarray(ss)


def _dft_rows(n):
    idx = np.arange(n, dtype=np.int64)
    ang = 2.0 * np.pi * ((idx[:, None] * idx[None, :]) % n).astype(np.float64) / n
    scale = 1.0 / np.sqrt(n)
    return np.cos(ang) * scale, np.sin(ang) * scale


def _fourier_consts(ctx_len, seq):
    cc, sc = _dft_rows(C_GDIM)
    cs = np.concatenate([cc, sc], axis=1)
    cl, sl = _dft_rows(seq)
    cx, sx = _dft_rows(ctx_len)
    return (jnp.asarray(cs, BF16), jnp.asarray(np.concatenate([cl, -sl], axis=1), BF16),
            jnp.asarray(np.concatenate([cx, -sx], axis=1), BF16))


def _pad_heads(w, width, pad_to):
    kdim = w.shape[0]
    w = w.reshape(kdim, -1, width)
    w = jnp.pad(w, ((0, 0), (0, 0), (0, pad_to - width)))
    return w.reshape(kdim, -1)


def _even_params(j, w_in, a_mu, a_w0, a_w2, a_a0, a_a2, a_g2, a_k_k, a_k_a, a_r_k, a_ln_w, a_ln_b,
                 b_q_norm, b_w_q_up, b_kv_norm, b_w_kv_up, b_q_gain, b_k_gain, w_out_even):
    w = jnp.pad(w_in[j], ((0, 0), (0, PB_COLS - B_COLS))).astype(BF16)

    def lora_pad(m):
        z = jnp.zeros_like(m[0])
        return jnp.stack([jnp.concatenate([m[0], z], axis=0), jnp.concatenate([z, m[1]], axis=0)])

    rw = {"mu": a_mu[j][None, :],
          "w0": a_w0[j][:, None, :], "w2": lora_pad(a_w2[j]),
          "a0": a_a0[j][:, None, :], "a2": lora_pad(a_a2[j]),
          "g2": a_g2[j], "kk": a_k_k[j][None, :], "ka": a_k_a[j][None, :],
          "rk": a_r_k[j].reshape(1, A_W), "lnw": a_ln_w[j][None, :], "lnb": a_ln_b[j][None, :]}
    kvu = b_w_kv_up[j].reshape(B_KV_RANK, B_HEADS, B_NOPE + B_V)
    wk_nope = _pad_heads(kvu[:, :, :B_NOPE].reshape(B_KV_RANK, -1), B_NOPE, HP)
    wv = kvu[:, :, B_NOPE:].reshape(B_KV_RANK, -1)
    sel = np.zeros((B_ROPE, B_HEADS, HP), np.float32)
    for h in range(B_HEADS):
        sel[np.arange(B_ROPE), h, B_NOPE + np.arange(B_ROPE)] = 1.0
    krows = PB_COLS - B_Q_RANK
    wk = jnp.zeros((krows, B_HEADS * HP + B_HEADS * B_V), F32)
    wk = wk.at[0:B_KV_RANK, 0:B_HEADS * HP].set(wk_nope)
    wk = wk.at[B_KV_RANK:B_KV_RANK + B_ROPE, 0:B_HEADS * HP].set(jnp.asarray(sel.reshape(B_ROPE, -1)))
    wk = wk.at[0:B_KV_RANK, B_HEADS * HP:].set(wv)
    mla = {"qn": b_q_norm[j][None, :],
           "kvn": jnp.pad(b_kv_norm[j], (0, krows - B_KV_RANK), constant_values=1.0)[None, :],
           "wq": _pad_heads(b_w_q_up[j], B_QK, HP).astype(BF16),
           "wk": wk.astype(BF16),
           "qg": jnp.pad(b_q_gain[j], (0, HP - B_QK))[None, :],
           "kg": jnp.pad(b_k_gain[j], (0, HP - B_QK))[None, :]}
    return w, rw, mla, w_out_even[j].astype(BF16)


def kernel(x, c, ctx, c_ctx, norm1_g, norm2_g, mod_w, mod_b, w_in, a_mu, a_w0, a_w2, a_a0, a_a2, a_g2,
           a_k_k, a_k_a, a_r_k, a_ln_w, a_ln_b, b_q_norm, b_w_q_up, b_kv_norm, b_w_kv_up, b_q_gain,
           b_k_gain, w_out_even, w_out_fourier, router_w, router_b, moe_w_gate, moe_w_up, moe_w_down,
           shared_w_gate, shared_w_up, shared_w_down):
    bsz, seq, d = x.shape
    ctx_len = ctx.shape[1]
    assert d == D and seq % TM == 0 and ctx_len % TM == 0 and seq % GRID_W == 0
    assert DEPTH % 2 == 0
    nct = ctx_len // TM
    xs = jnp.concatenate([ctx, x], axis=1)
    n_all = bsz * (ctx_len + seq)
    n_slots = (n_all + N_CLASS * (MB - 1) + MB - 1) // MB * MB
    xs_buf = jnp.zeros((n_slots, D + EXT), F32)

    rows = (bsz + 1 + 7) // 8 * 8
    cs = jnp.zeros((rows, D), F32).at[0:bsz].set(c).at[bsz].set(c_ctx)
    mod = _modulation(cs, mod_w, mod_b)

    def table(part):
        lat = part[0:bsz]
        cx = jnp.broadcast_to(part[bsz][None, :], (bsz, D))
        return jnp.stack([cx, lat], axis=1)[:, :, None, :]

    consts = _rwkv_consts()
    rope = _rope_tables(ctx_len, seq)
    cs_mat, dlat, dctx = _fourier_consts(ctx_len, seq)
    rt = _route_consts(router_w, router_b)
    nctx_chunks = ctx_len // CH

    for i in range(DEPTH):
        j = i // 2
        t0 = nct if i == DEPTH - 1 else 0
        sh1, sc1, g1, sh2, sc2, g2 = [mod[i][:, D * q:D * (q + 1)] for q in range(6)]
        t_sc1 = table((1.0 + sc1) * norm1_g[i][None, :])
        t_sh1 = table(sh1)
        t_g1 = table(g1)
        t_sc2 = table((1.0 + sc2) * norm2_g[i][None, :])
        t_sh2 = table(sh2)
        t_g2 = table(g2)
        if i % 2 == 0:
            w, rw, mla, wo = _even_params(
                j, w_in, a_mu, a_w0, a_w2, a_a0, a_a2, a_g2, a_k_k, a_k_a, a_r_k, a_ln_w, a_ln_b,
                b_q_norm, b_w_q_up, b_kv_norm, b_w_kv_up, b_q_gain, b_k_gain, w_out_even)
            pa, pb = _in_proj(xs, t_sc1, t_sh1, w, nct)
            yf = _rwkv_pass(pa, rw, consts, None, reverse=False, nctx=nctx_chunks)
            ya = _rwkv_pass(pa, rw, consts, yf, reverse=True, nctx=nctx_chunks)
            q, k, v = _mla_prep(pb, mla, rope)
            yb = _attention(q, k, v, ctx_len)
            xs, he, cnt = _even_out(ya, yb, wo, xs, t_g1, t_sc2, t_sh2, rt, nct)
        else:
            hc, hs = _fourier_channels(xs, t_sc1, t_sh1, cs_mat, nct)
            xs, he, cnt = _fourier_out(hc, hs, dlat, dctx, w_out_fourier[j].astype(BF16), xs,
                                       t_g1, t_sc2, t_sh2, rt, ctx_len, t0)
        xs, xs_buf = _moe_layer(
            xs, he, cnt, xs_buf, t_g2,
            moe_w_gate[i].astype(BF16), moe_w_up[i].astype(BF16), moe_w_down[i].astype(BF16),
            shared_w_gate[i].astype(BF16), shared_w_up[i].astype(BF16),
            shared_w_down[i].astype(BF16), nct, t0)
    return xs
```
